```python
import jax, jax.numpy as jnp
from jax import lax
import numpy as np

D_MODEL = 1024
BATCH = 1
SEQ = 16384
DEPTH = 1
DEC_BATCH = 32
DEC_SEQ = 8
PAST_LEN = 16384
PAGE_SIZE = 128

N_META = 16
ATTN_HEADS = 16
ATTN_HEAD_DIM = 64
ATTN_WIDTH = ATTN_HEADS * ATTN_HEAD_DIM
ATTN_SCALE = ATTN_HEAD_DIM ** -0.5
Q_BLOCK = 128
LOGIT_BIAS_MIN = -12.0
LOGIT_BIAS_MAX = -8.0
SSD_INNER = 2 * D_MODEL
SSD_HEAD_DIM = 64
SSD_HEADS = SSD_INNER // SSD_HEAD_DIM
SSD_GROUPS = 4
D_STATE = 128
CONV_W = 4
CONV_DIM = SSD_INNER + 2 * SSD_GROUPS * D_STATE
SSD_CHUNK = 128
MIX_WIDTH = ATTN_WIDTH + SSD_INNER
D_IN_PROJ = 3 * ATTN_WIDTH + SSD_INNER + CONV_DIM + SSD_HEADS
IN_SPLITS = (ATTN_WIDTH, 2 * ATTN_WIDTH, 3 * ATTN_WIDTH,
             3 * ATTN_WIDTH + SSD_INNER, 3 * ATTN_WIDTH + SSD_INNER + CONV_DIM)
N_EXPERT_GROUPS = 4
EXPERTS_PER_GROUP = 8
N_EXPERTS = N_EXPERT_GROUPS * EXPERTS_PER_GROUP
TOP_K = 2
D_EXPERT = 512
EPS = 1e-6

kernel_name = "hymba_sb_ssd_hmoe_step"


def rmsnorm(x, w):
    xf = x.astype(jnp.float32)
    r = lax.rsqrt(jnp.mean(xf * xf, axis=-1, keepdims=True) + EPS)
    return (xf * r).astype(x.dtype) * w


def sb_attend(q, q_pos, k, v, k_pos, bias):
    z = jnp.einsum('bqhd,bkhd->bhqk', q.astype(jnp.float32), k.astype(jnp.float32)) * ATTN_SCALE
    z = z + bias.astype(jnp.float32)[None, :, None, None]
    mask = k_pos[None, :] < q_pos[:, None]
    log_keep = jnp.where(mask, jax.nn.log_sigmoid(-z), 0.0)
    later = lax.cumsum(log_keep, axis=3, reverse=True) - log_keep
    w = jnp.where(mask, jnp.exp(jax.nn.log_sigmoid(z) + later), 0.0)
    return jnp.einsum('bhqk,bkhd->bqhd', w, v.astype(jnp.float32)).astype(q.dtype)


def sb_prompt(q, k, v, bias):
    b, T, H, Dh = q.shape
    pos = jnp.arange(T)
    out_meta = sb_attend(q[:, :N_META], pos[:N_META], k[:, :N_META], v[:, :N_META], pos[:N_META], bias)
    n_blk = (T - N_META) // Q_BLOCK
    qb = jnp.moveaxis(q[:, N_META:].reshape(b, n_blk, Q_BLOCK, H, Dh), 1, 0)
    pb = pos[N_META:].reshape(n_blk, Q_BLOCK)
    ob = lax.map(lambda a: sb_attend(a[0], a[1], k, v, pos, bias), (qb, pb))
    out_real = jnp.moveaxis(ob, 0, 1).reshape(b, T - N_META, H, Dh)
    return jnp.concatenate([out_meta, out_real], axis=1)


def sb_sample(q, k_new, v_new, cache_k, cache_v, page_table, bias):
    past = page_table.shape[1] * PAGE_SIZE
    tn = q.shape[1]
    q_pos = past + jnp.arange(tn)
    k_pos = jnp.arange(past + tn)

    def one(args):
        qs, ks, vs, pages = args
        kp = jnp.take(cache_k, pages, axis=0).reshape(past, ATTN_HEADS, ATTN_HEAD_DIM)
        vp = jnp.take(cache_v, pages, axis=0).reshape(past, ATTN_HEADS, ATTN_HEAD_DIM)
        kk = jnp.concatenate([kp.astype(ks.dtype), ks], axis=0)[None]
        vv = jnp.concatenate([vp.astype(vs.dtype), vs], axis=0)[None]
        return sb_attend(qs[None], q_pos, kk, vv, k_pos, bias)[0]

    return lax.map(one, (q, k_new, v_new, page_table))


def causal_conv(xbc, buf, conv_w, conv_b):
    xp = jnp.concatenate([buf.astype(xbc.dtype), xbc], axis=1)
    L = xbc.shape[1]
    out = conv_b + xp[:, 0:L] * conv_w[0]
    for i in range(1, CONV_W):
        out = out + xp[:, i:i + L] * conv_w[i]
    return jax.nn.silu(out), xp[:, -(CONV_W - 1):]


def ssd_chunked(x, dt, a, B, C, h0, chunk):
    f32 = jnp.float32
    b, L, H, P = x.shape
    nc = L // chunk
    R = SSD_HEADS // SSD_GROUPS
    xdt = (x.astype(f32) * dt[..., None]).reshape(b, nc, chunk, SSD_GROUPS, R, P)
    la = (dt * a).reshape(b, nc, chunk, SSD_GROUPS, R)
    Bc = B.astype(f32).reshape(b, nc, chunk, SSD_GROUPS, D_STATE)
    Cc = C.astype(f32).reshape(b, nc, chunk, SSD_GROUPS, D_STATE)
    cum = jnp.cumsum(la, axis=2)
    causal = jnp.tril(jnp.ones((chunk, chunk), dtype=bool))[:, :, None, None]
    seg = cum[:, :, :, None] - cum[:, :, None, :]
    decay = jnp.where(causal, jnp.exp(jnp.where(causal, seg, 0.0)), 0.0)
    cb = jnp.einsum('bctgn,bcsgn->bctsg', Cc, Bc)
    y_diag = jnp.einsum('bctsg,bctsgr,bcsgrp->bctgrp', cb, decay, xdt)
    to_end = jnp.exp(cum[:, :, -1:] - cum)
    chunk_states = jnp.einsum('bcsgn,bcsgr,bcsgrp->bcgrpn', Bc, to_end, xdt)
    chunk_decay = jnp.exp(cum[:, :, -1])

    def step(h, inp):
        dec, st = inp
        return dec[..., None, None] * h + st, h

    h_final, h_prev = lax.scan(step, h0.astype(f32).reshape(b, SSD_GROUPS, R, P, D_STATE),
                               (jnp.moveaxis(chunk_decay, 1, 0), jnp.moveaxis(chunk_states, 1, 0)))
    h_prev = jnp.moveaxis(h_prev, 0, 1)
    y_off = jnp.einsum('bctgn,bcgrpn,bctgr->bctgrp', Cc, h_prev, jnp.exp(cum))
    y = (y_diag + y_off).reshape(b, L, H, P)
    return y, h_final.reshape(b, H, P, D_STATE)


def ssd_mixer(z, xbc_raw, dt_raw, conv_buf, h0, segments, lp):
    f32 = jnp.float32
    b, L, _ = z.shape
    xbc, new_buf = causal_conv(xbc_raw, conv_buf, lp['conv_w'], lp['conv_b'])
    xs, Bm, Cm = jnp.split(xbc, [SSD_INNER, SSD_INNER + SSD_GROUPS * D_STATE], axis=-1)
    xs = xs.reshape(b, L, SSD_HEADS, SSD_HEAD_DIM)
    Bm = Bm.reshape(b, L, SSD_GROUPS, D_STATE)
    Cm = Cm.reshape(b, L, SSD_GROUPS, D_STATE)
    dt = jax.nn.softplus(dt_raw.astype(f32) + lp['dt_bias'].astype(f32))
    a = -jnp.exp(lp['a_log'].astype(f32))
    ys = []
    h = h0
    start = 0
    for length, chunk in segments:
        sl = slice(start, start + length)
        y_seg, h = ssd_chunked(xs[:, sl], dt[:, sl], a, Bm[:, sl], Cm[:, sl], h, chunk)
        ys.append(y_seg)
        start += length
    y = jnp.concatenate(ys, axis=1) + xs.astype(f32) * lp['d_skip'].astype(f32)[:, None]
    y = y.reshape(b, L, SSD_INNER) * jax.nn.silu(z.astype(f32))
    yg = y.reshape(b, L, SSD_GROUPS, SSD_INNER // SSD_GROUPS)
    yg = yg * lax.rsqrt(jnp.mean(yg * yg, axis=-1, keepdims=True) + EPS)
    y = yg.reshape(b, L, SSD_INNER).astype(z.dtype) * lp['ssd_norm_w']
    return y, h, new_buf


def hier_moe(x, lp):
    f32 = jnp.float32
    b, L, D = x.shape
    xt = x.reshape(-1, D)
    T = xt.shape[0]
    g_logits = (xt @ lp['router_group_w'] + lp['router_group_b']).astype(f32)
    g_prob = jax.nn.softmax(g_logits, axis=-1)
    g_sel = jnp.argmax(g_logits, axis=-1)
    g_w = jnp.take_along_axis(g_prob, g_sel[:, None], axis=-1)
    e_logits = (xt @ lp['router_expert_w'] + lp['router_expert_b']).astype(f32)
    e_logits = e_logits.reshape(T, N_EXPERT_GROUPS, EXPERTS_PER_GROUP)
    e_in = jnp.take_along_axis(e_logits, g_sel[:, None, None], axis=1)[:, 0]
    top_v, top_i = lax.top_k(e_in, TOP_K)
    top_w = jax.nn.softmax(top_v, axis=-1) * g_w
    expert_id = g_sel[:, None] * EXPERTS_PER_GROUP + top_i
    combine = jnp.sum(jax.nn.one_hot(expert_id, N_EXPERTS, dtype=f32) * top_w[..., None], axis=1)
    y = jnp.zeros((T, D), f32)
    for g in range(N_EXPERT_GROUPS):
        sl = slice(g * EXPERTS_PER_GROUP, (g + 1) * EXPERTS_PER_GROUP)
        hg = jax.nn.silu(jnp.einsum('td,edf->tef', xt, lp['expert_w_gate'][sl])) * \
            jnp.einsum('td,edf->tef', xt, lp['expert_w_up'][sl])
        y = y + jnp.einsum('tef,te,efd->td', hg.astype(f32), combine[:, sl],
                           lp['expert_w_down'][sl].astype(f32))
    return y.astype(x.dtype).reshape(b, L, D)


def hybrid_layer(x, lp, attend_fn, conv_buf, ssm_h0, segments):
    b, L, _ = x.shape
    xn = rmsnorm(x, lp['norm_mix_w'])
    proj = xn @ lp['w_in']
    q, k, v, z, xbc, dt_raw = jnp.split(proj, IN_SPLITS, axis=-1)
    heads = (b, L, ATTN_HEADS, ATTN_HEAD_DIM)
    q, k, v = q.reshape(heads), k.reshape(heads), v.reshape(heads)
    attn = attend_fn(q, k, v, lp['attn_logit_bias']).reshape(b, L, ATTN_WIDTH)
    attn = rmsnorm(attn, lp['attn_norm_w'])
    ssd, h_new, buf_new = ssd_mixer(z, xbc, dt_raw, conv_buf, ssm_h0, segments, lp)
    x = x + jnp.concatenate([attn, ssd], axis=-1) @ lp['w_out']
    x = x + hier_moe(rmsnorm(x, lp['norm_ffn_w']), lp)
    return x, k, v, h_new, buf_new


def setup_inputs(seed: int = 0) -> dict:
    key = jax.random.key(seed)
    ks = jax.random.split(key, 32)
    f32 = jnp.float32
    n_pages = PAST_LEN // PAGE_SIZE
    n_used = DEC_BATCH * n_pages
    n_pool = n_used + max(1, n_used // 4)
    nrm = lambda k, shape, s: jax.random.normal(k, shape, f32) * s
    x_prompt = nrm(ks[0], (BATCH, SEQ, D_MODEL), 1.0)
    x_sample = nrm(ks[1], (DEC_BATCH, DEC_SEQ, D_MODEL), 1.0)
    cache_k = nrm(ks[2], (DEPTH, n_pool, PAGE_SIZE, ATTN_HEADS, ATTN_HEAD_DIM), 1.0)
    cache_v = nrm(ks[3], (DEPTH, n_pool, PAGE_SIZE, ATTN_HEADS, ATTN_HEAD_DIM), 1.0)
    page_table = jax.random.permutation(ks[4], n_pool)[:n_used].reshape(DEC_BATCH, n_pages).astype(jnp.int32)
    state_ssm = nrm(ks[5], (DEPTH, DEC_BATCH, SSD_HEADS, SSD_HEAD_DIM, D_STATE), 0.1)
    state_conv = nrm(ks[6], (DEPTH, DEC_BATCH, CONV_W - 1, CONV_DIM), 1.0)
    meta_tokens = nrm(ks[7], (N_META, D_MODEL), 1.0)
    gain = lambda k, shape: 1.0 + nrm(k, shape, 0.01)
    norm_mix_w = gain(ks[8], (DEPTH, D_MODEL))
    w_in = nrm(ks[9], (DEPTH, D_MODEL, D_IN_PROJ), D_MODEL ** -0.5)
    attn_logit_bias = jax.random.uniform(ks[27], (DEPTH, ATTN_HEADS), f32,
                                         minval=LOGIT_BIAS_MIN, maxval=LOGIT_BIAS_MAX)
    conv_w = nrm(ks[10], (DEPTH, CONV_W, CONV_DIM), CONV_W ** -0.5)
    conv_b = nrm(ks[11], (DEPTH, CONV_DIM), 0.01)
    dt0 = jnp.exp(jax.random.uniform(ks[12], (DEPTH, SSD_HEADS), f32,
                                     minval=float(np.log(1e-3)), maxval=float(np.log(1e-1))))
    dt_bias = dt0 + jnp.log(-jnp.expm1(-dt0))
    a_log = jnp.log(jax.random.uniform(ks[13], (DEPTH, SSD_HEADS), f32, minval=1.0, maxval=16.0))
    d_skip = gain(ks[14], (DEPTH, SSD_HEADS))
    ssd_norm_w = gain(ks[15], (DEPTH, SSD_INNER))
    attn_norm_w = gain(ks[16], (DEPTH, ATTN_WIDTH))
    w_out = nrm(ks[17], (DEPTH, MIX_WIDTH, D_MODEL), MIX_WIDTH ** -0.5)
    norm_ffn_w = gain(ks[18], (DEPTH, D_MODEL))
    router_group_w = nrm(ks[19], (DEPTH, D_MODEL, N_EXPERT_GROUPS), D_MODEL ** -0.5)
    router_group_b = nrm(ks[20], (DEPTH, N_EXPERT_GROUPS), 0.01)
    router_expert_w = nrm(ks[21], (DEPTH, D_MODEL, N_EXPERTS), D_MODEL ** -0.5)
    router_expert_b = nrm(ks[22], (DEPTH, N_EXPERTS), 0.01)
    expert_w_gate = nrm(ks[23], (DEPTH, N_EXPERTS, D_MODEL, D_EXPERT), D_MODEL ** -0.5)
    expert_w_up = nrm(ks[24], (DEPTH, N_EXPERTS, D_MODEL, D_EXPERT), D_MODEL ** -0.5)
    expert_w_down = nrm(ks[25], (DEPTH, N_EXPERTS, D_EXPERT, D_MODEL), D_EXPERT ** -0.5)
    final_norm_w = gain(ks[26], (D_MODEL,))
    return {"x_prompt": x_prompt, "x_sample": x_sample, "cache_k": cache_k, "cache_v": cache_v,
            "page_table": page_table, "state_ssm": state_ssm, "state_conv": state_conv,
            "meta_tokens": meta_tokens, "norm_mix_w": norm_mix_w, "w_in": w_in,
            "attn_logit_bias": attn_logit_bias, "conv_w": conv_w,
            "conv_b": conv_b, "dt_bias": dt_bias, "a_log": a_log, "d_skip": d_skip,
            "ssd_norm_w": ssd_norm_w, "attn_norm_w": attn_norm_w, "w_out": w_out,
            "norm_ffn_w": norm_ffn_w, "router_group_w": router_group_w, "router_group_b": router_group_b,
            "router_expert_w": router_expert_w, "router_expert_b": router_expert_b,
            "expert_w_gate": expert_w_gate, "expert_w_up": expert_w_up, "expert_w_down": expert_w_down,
            "final_norm_w": final_norm_w}


def reference(x_prompt, x_sample, cache_k, cache_v, page_table, state_ssm, state_conv,
              meta_tokens, norm_mix_w, w_in, attn_logit_bias, conv_w, conv_b, dt_bias, a_log, d_skip,
              ssd_norm_w, attn_norm_w, w_out, norm_ffn_w, router_group_w, router_group_b,
              router_expert_w, router_expert_b, expert_w_gate, expert_w_up, expert_w_down,
              final_norm_w):
    b_p, seq_p, _ = x_prompt.shape
    meta = jnp.broadcast_to(meta_tokens[None].astype(x_prompt.dtype), (b_p, N_META, D_MODEL))
    xp = jnp.concatenate([meta, x_prompt], axis=1)
    xs = x_sample
    prompt_segments = [(N_META, N_META), (seq_p, SSD_CHUNK)]
    sample_segments = [(x_sample.shape[1], x_sample.shape[1])]
    kp_l, vp_l, hp_l, bp_l, ks_l, vs_l, hs_l, bs_l = [], [], [], [], [], [], [], []
    for l in range(DEPTH):
        lp = dict(norm_mix_w=norm_mix_w[l], w_in=w_in[l], attn_logit_bias=attn_logit_bias[l],
                  conv_w=conv_w[l], conv_b=conv_b[l],
                  dt_bias=dt_bias[l], a_log=a_log[l], d_skip=d_skip[l], ssd_norm_w=ssd_norm_w[l],
                  attn_norm_w=attn_norm_w[l], w_out=w_out[l], norm_ffn_w=norm_ffn_w[l],
                  router_group_w=router_group_w[l], router_group_b=router_group_b[l],
                  router_expert_w=router_expert_w[l], router_expert_b=router_expert_b[l],
                  expert_w_gate=expert_w_gate[l], expert_w_up=expert_w_up[l],
                  expert_w_down=expert_w_down[l])
        zero_buf = jnp.zeros((b_p, CONV_W - 1, CONV_DIM), xp.dtype)
        zero_h = jnp.zeros((b_p, SSD_HEADS, SSD_HEAD_DIM, D_STATE), jnp.float32)
        xp, kp, vp, hp, bp = hybrid_layer(xp, lp, sb_prompt, zero_buf, zero_h, prompt_segments)
        ck, cv = cache_k[l], cache_v[l]
        attend_s = lambda q, k, v, bias, ck=ck, cv=cv: sb_sample(q, k, v, ck, cv, page_table, bias)
        xs, ksm, vsm, hsm, bsm = hybrid_layer(xs, lp, attend_s, state_conv[l], state_ssm[l], sample_segments)
        kp_l.append(kp); vp_l.append(vp); hp_l.append(hp); bp_l.append(bp)
        ks_l.append(ksm); vs_l.append(vsm); hs_l.append(hsm); bs_l.append(bsm)
    y_prompt = rmsnorm(xp, final_norm_w)[:, N_META:]
    y_sample = rmsnorm(xs, final_norm_w)
    return (y_prompt, y_sample,
            jnp.stack(kp_l), jnp.stack(vp_l), jnp.stack(hp_l).astype(state_ssm.dtype), jnp.stack(bp_l),
            jnp.stack(ks_l), jnp.stack(vs_l), jnp.stack(hs_l).astype(state_ssm.dtype), jnp.stack(bs_l))
```

```python
import functools

import jax
import jax.numpy as jnp
from jax import lax
from jax.experimental import pallas as pl
from jax.experimental.pallas import tpu as pltpu

F32 = jnp.float32
BF16 = jnp.bfloat16

LANES = 128
SUBLANES = 8
VMEM_LIMIT = 56 * 1024 * 1024

N_META = 16
D_MODEL = 1024
ATTN_HEADS = 16
ATTN_HEAD_DIM = 64
ATTN_WIDTH = ATTN_HEADS * ATTN_HEAD_DIM
ATTN_SCALE = ATTN_HEAD_DIM ** -0.5
HEAD_PAIRS = ATTN_HEADS // 2
SSD_INNER = 2 * D_MODEL
SSD_HEAD_DIM = 64
SSD_HEADS = SSD_INNER // SSD_HEAD_DIM
SSD_GROUPS = 4
D_STATE = 128
CONV_W = 4
GROUP_BC = SSD_GROUPS * D_STATE
CONV_DIM = SSD_INNER + 2 * GROUP_BC
N_EXPERT_GROUPS = 4
EXPERTS_PER_GROUP = 8
N_EXPERTS = N_EXPERT_GROUPS * EXPERTS_PER_GROUP
D_EXPERT = 512
PAGE_SIZE = 128
EPS = 1e-6

Q_BLOCK = 256
KEY_TILE = 128
STREAM = KEY_TILE // SUBLANES
CHUNK = 128
PROMPT_PAD = Q_BLOCK - N_META
PAGES_PER_STEP = 8


def _rms_scale(x):
    return lax.rsqrt(jnp.mean(x * x, axis=-1, keepdims=True) + EPS)


def _cparams(sem):
    return pltpu.CompilerParams(dimension_semantics=sem, vmem_limit_bytes=VMEM_LIMIT)


def _inproj_kernel(x_ref, nw_ref, w_ref, wdt_ref,
                   q_ref, k32_ref, v32_ref, kb_ref, vb_ref, z_ref, xbc_ref, dt_ref, xn_ref):
    j = pl.program_id(1)

    @pl.when(j == 0)
    def _():
        x = x_ref[...]
        xn_ref[...] = ((x * _rms_scale(x)) * nw_ref[...]).astype(BF16)

    res = jnp.dot(xn_ref[...], w_ref[...], preferred_element_type=F32)

    @pl.when(j == 0)
    def _():
        q_ref[...] = (res * ATTN_SCALE).astype(BF16)

    @pl.when(j == 1)
    def _():
        k32_ref[...] = res
        kb_ref[...] = res.astype(BF16)

    @pl.when(j == 2)
    def _():
        v32_ref[...] = res
        vb_ref[...] = res.astype(BF16)

    @pl.when((j == 3) | (j == 4))
    def _():
        z_ref[...] = res

    @pl.when(j >= 5)
    def _():
        xbc_ref[...] = res

    @pl.when(j == 7)
    def _():
        dt_ref[...] = jnp.dot(xn_ref[...], wdt_ref[...], preferred_element_type=F32)


def _in_proj(x, norm_w, w_main, w_dt, tm):
    t = x.shape[0]
    nb = D_MODEL
    n_col = w_main.shape[1] // nb
    row = lambda i, j: (i, 0)
    out_shape = (
        jax.ShapeDtypeStruct((t, ATTN_WIDTH), BF16),
        jax.ShapeDtypeStruct((t, ATTN_WIDTH), F32),
        jax.ShapeDtypeStruct((t, ATTN_WIDTH), F32),
        jax.ShapeDtypeStruct((t, ATTN_WIDTH), BF16),
        jax.ShapeDtypeStruct((t, ATTN_WIDTH), BF16),
        jax.ShapeDtypeStruct((t, SSD_INNER), F32),
        jax.ShapeDtypeStruct((t, CONV_DIM), F32),
        jax.ShapeDtypeStruct((t, LANES), F32),
    )
    out_specs = (
        pl.BlockSpec((tm, nb), row), pl.BlockSpec((tm, nb), row), pl.BlockSpec((tm, nb), row),
        pl.BlockSpec((tm, nb), row), pl.BlockSpec((tm, nb), row),
        pl.BlockSpec((tm, nb), lambda i, j: (i, jnp.clip(j - 3, 0, 1))),
        pl.BlockSpec((tm, nb), lambda i, j: (i, jnp.clip(j - 5, 0, 2))),
        pl.BlockSpec((tm, LANES), row),
    )
    return pl.pallas_call(
        _inproj_kernel,
        grid=(t // tm, n_col),
        in_specs=[
            pl.BlockSpec((tm, D_MODEL), row),
            pl.BlockSpec((1, D_MODEL), lambda i, j: (0, 0)),
            pl.BlockSpec((D_MODEL, nb), lambda i, j: (0, j)),
            pl.BlockSpec((D_MODEL, LANES), lambda i, j: (0, 0)),
        ],
        out_specs=out_specs,
        out_shape=out_shape,
        scratch_shapes=[pltpu.VMEM((tm, D_MODEL), BF16)],
        compiler_params=_cparams(("arbitrary", "arbitrary")),
        name="in_proj",
    )(x, norm_w, w_main, w_dt)


def _key_in_tile(v, n):
    return STREAM * lax.broadcasted_iota(jnp.int32, (SUBLANES, n), 0) + v


def _sublane_suffix(tot):
    sub = lax.broadcasted_iota(jnp.int32, tot.shape, 0)
    s = tot
    for k in (1, 2, 4):
        rolled = pltpu.roll(s, SUBLANES - k, 0)
        s = s + jnp.where(sub < SUBLANES - k, rolled, 0.0)
    return s


def _sb_tile(s, bias, carry, valid):
    n = s.shape[1]
    nv = KEY_TILE // SUBLANES
    run = jnp.zeros((SUBLANES, n), F32)
    part = [None] * nv
    for v in range(nv - 1, -1, -1):
        rows = slice(SUBLANES * v, SUBLANES * (v + 1))
        z = s[rows] + bias
        neg = -z
        lk = jnp.minimum(neg, 0.0) - jnp.log(1.0 + jnp.exp(jnp.minimum(z, neg)))
        if valid is not None:
            lk = jnp.where(valid(v), lk, 0.0)
        part[v] = z + lk + run
        run = run + lk
    incl = _sublane_suffix(run)
    base = (incl - run) + carry
    new_carry = carry + jnp.broadcast_to(incl[0:1], carry.shape)
    ws = []
    for v in range(nv):
        w = jnp.exp(part[v] + base)
        if valid is not None:
            w = jnp.where(valid(v), w, 0.0)
        ws.append(w)
    return jnp.concatenate(ws, axis=0), new_carry


def _prompt_attn_kernel(bias_ref, qt_ref, k_ref, vt_ref, o_ref, acc_ref, carry_ref):
    p = pl.program_id(0)
    i = pl.program_id(1)
    tiles_per_q = Q_BLOCK // KEY_TILE
    q2 = qt_ref[...]
    row = lax.broadcasted_iota(jnp.int32, q2.shape, 0)
    qpos = i * Q_BLOCK + lax.broadcasted_iota(jnp.int32, (SUBLANES, LANES), 1)

    for hh in range(2):
        bias = bias_ref[2 * p + hh]
        qpad = jnp.where((row // ATTN_HEAD_DIM) == hh, q2, jnp.zeros_like(q2))
        acc_rows = slice(ATTN_HEAD_DIM * hh, ATTN_HEAD_DIM * (hh + 1))
        acc_ref[acc_rows, :] = jnp.zeros((ATTN_HEAD_DIM, Q_BLOCK), F32)
        carry_ref[...] = jnp.zeros_like(carry_ref)

        def tile(kt, masked):
            s = jnp.dot(k_ref[0, kt], qpad, preferred_element_type=F32)
            halves, carries = [], []
            for c in range(Q_BLOCK // LANES):
                cols = slice(LANES * c, LANES * (c + 1))
                valid = None
                if masked:
                    def valid(v, c=c):
                        kpos = kt * KEY_TILE + _key_in_tile(v, LANES)
                        return (kpos < qpos + LANES * c) & (kpos >= PROMPT_PAD)
                w, nc = _sb_tile(s[:, cols], bias, carry_ref[:, cols], valid)
                halves.append(w)
                carries.append(nc)
            carry_ref[...] = jnp.concatenate(carries, axis=1)
            wt = jnp.concatenate(halves, axis=1).astype(BF16)
            vt = vt_ref[0, kt, acc_rows, :]
            acc_ref[acc_rows, :] += jnp.dot(vt, wt, preferred_element_type=F32)

        @pl.when(i > 0)
        def _():
            def diag(n, c):
                tile(tiles_per_q * i + (tiles_per_q - 1) - n, True)
                return c
            lax.fori_loop(0, tiles_per_q, diag, 0)

            def full(n, c):
                tile(tiles_per_q * i - 1 - n, False)
                return c
            lax.fori_loop(0, tiles_per_q * (i - 1), full, 0)

        tile(1, True)

    o_ref[...] = acc_ref[...].T


def _prompt_attention(bias, q_t, k_perm, v_t):
    t = q_t.shape[1]
    n_kt = t // KEY_TILE
    pair_w = 2 * ATTN_HEAD_DIM
    return pl.pallas_call(
        _prompt_attn_kernel,
        grid=(HEAD_PAIRS, t // Q_BLOCK),
        in_specs=[
            pl.BlockSpec(memory_space=pltpu.SMEM),
            pl.BlockSpec((pair_w, Q_BLOCK), lambda p, i: (p, i)),
            pl.BlockSpec((1, n_kt, KEY_TILE, pair_w), lambda p, i: (p, 0, 0, 0)),
            pl.BlockSpec((1, n_kt, pair_w, KEY_TILE), lambda p, i: (p, 0, 0, 0)),
        ],
        out_specs=pl.BlockSpec((Q_BLOCK, pair_w), lambda p, i: (i, p)),
        out_shape=jax.ShapeDtypeStruct((t, ATTN_WIDTH), F32),
        scratch_shapes=[pltpu.VMEM((pair_w, Q_BLOCK), F32), pltpu.VMEM((SUBLANES, Q_BLOCK), F32)],
        compiler_params=_cparams(("arbitrary", "arbitrary")),
        name="prompt_attention",
    )(bias, q_t, k_perm, v_t)


def _sample_attn_kernel(pt_ref, qbd_ref, bias_ref, knew_ref, vnew_ref, *rest, dec_seq):
    n_pg = PAGES_PER_STEP
    k_refs = rest[:n_pg]
    v_refs = rest[n_pg:2 * n_pg]
    o_ref, acc_ref, carry_ref = rest[2 * n_pg:]
    s_id = pl.program_id(1)
    qbd = qbd_ref[0]
    bias = bias_ref[...]
    r_io = lax.broadcasted_iota(jnp.int32, (KEY_TILE, LANES), 0)
    c_io = lax.broadcasted_iota(jnp.int32, (KEY_TILE, LANES), 1)
    later_or_same = (c_io >= r_io).astype(BF16)

    def tile(k_rows, v_rows, valid):
        z = jnp.dot(k_rows.astype(BF16), qbd, preferred_element_type=F32) + bias
        neg = -z
        lk = jnp.minimum(neg, 0.0) - jnp.log(1.0 + jnp.exp(jnp.minimum(z, neg)))
        if valid is not None:
            lk = jnp.where(valid, lk, 0.0)
        incl = sum(jnp.dot(later_or_same, part, preferred_element_type=F32) for part in _split3(lk))
        carry = carry_ref[...]
        w = jnp.exp(z + (incl + carry))
        if valid is not None:
            w = jnp.where(valid, w, 0.0)
        carry_ref[...] = carry + incl[0:1, :]
        acc_ref[...] += jnp.dot(w.T.astype(BF16), v_rows.astype(BF16), preferred_element_type=F32)

    @pl.when(s_id == 0)
    def _():
        acc_ref[...] = jnp.zeros_like(acc_ref)
        carry_ref[...] = jnp.zeros_like(carry_ref)
        tile(knew_ref[0], vnew_ref[0], r_io < c_io % dec_seq)

    for r in range(n_pg):
        tile(k_refs[r][0], v_refs[r][0], None)

    @pl.when(s_id == pl.num_programs(1) - 1)
    def _():
        lane = lax.broadcasted_iota(jnp.int32, (dec_seq, ATTN_WIDTH), 1)
        out = jnp.zeros((dec_seq, ATTN_WIDTH), F32)
        for h in range(ATTN_HEADS):
            blk = acc_ref[dec_seq * h:dec_seq * (h + 1), :]
            out = out + jnp.where(lane // ATTN_HEAD_DIM == h, blk, 0.0)
        o_ref[0] = out


def _sample_attention(page_table, q_bd, bias_lanes, k_new_pad, v_new_pad, cache_k, cache_v, dec_seq):
    b, n_pages = page_table.shape
    n_steps = n_pages // PAGES_PER_STEP

    def page_spec(r):
        def idx(bi, si, pt):
            return (pt[bi, n_pages - 1 - (si * PAGES_PER_STEP + r)], 0, 0)
        return pl.BlockSpec((1, PAGE_SIZE, ATTN_WIDTH), idx)

    per_seq = lambda bi, si, pt: (bi, 0, 0)
    grid_spec = pltpu.PrefetchScalarGridSpec(
        num_scalar_prefetch=1,
        grid=(b, n_steps),
        in_specs=[
            pl.BlockSpec((1, ATTN_WIDTH, LANES), per_seq),
            pl.BlockSpec((1, LANES), lambda bi, si, pt: (0, 0)),
            pl.BlockSpec((1, KEY_TILE, ATTN_WIDTH), per_seq),
            pl.BlockSpec((1, KEY_TILE, ATTN_WIDTH), per_seq),
        ] + [page_spec(r) for r in range(PAGES_PER_STEP)] * 2,
        out_specs=pl.BlockSpec((1, dec_seq, ATTN_WIDTH), per_seq),
        scratch_shapes=[pltpu.VMEM((LANES, ATTN_WIDTH), F32), pltpu.VMEM((1, LANES), F32)],
    )
    return pl.pallas_call(
        functools.partial(_sample_attn_kernel, dec_seq=dec_seq),
        grid_spec=grid_spec,
        out_shape=jax.ShapeDtypeStruct((b, dec_seq, ATTN_WIDTH), F32),
        compiler_params=_cparams(("arbitrary", "arbitrary")),
        name="sample_attention",
    )(page_table, q_bd, bias_lanes, k_new_pad, v_new_pad,
      *([cache_k] * PAGES_PER_STEP), *([cache_v] * PAGES_PER_STEP))


def _split3(x):
    hi = x.astype(BF16)
    r1 = x - hi.astype(F32)
    mid = r1.astype(BF16)
    lo = (r1 - mid.astype(F32)).astype(BF16)
    return hi, mid, lo


def _ssd_kernel(xbc_ref, z_ref, dt_ref, h0_ref, hist_ref, convw_ref, convb_ref, dtb_ref, a_ref,
                dskip_ref, nw_ref, y_ref, hout_ref, buf_ref, xc_ref, dtbuf_ref, ybuf_ref, h_ref,
                *, rows, first_valid):
    c = pl.program_id(1)
    hist = SUBLANES

    @pl.when(c == 0)
    def _():
        h_ref[...] = h0_ref[0]
        buf_ref[...] = jnp.zeros_like(buf_ref)
        buf_ref[0:hist, :] = hist_ref[0]
        dtbuf_ref[...] = jnp.zeros_like(dtbuf_ref)

    buf_ref[hist:hist + rows, :] = xbc_ref[0]
    dtbuf_ref[0:rows, :] = dt_ref[0]

    conv = convb_ref[...]
    for tap in range(CONV_W):
        start = hist - (CONV_W - 1) + tap
        conv = conv + buf_ref[start:start + CHUNK, :] * convw_ref[tap:tap + 1, :]
    xc_ref[...] = conv * jax.nn.sigmoid(conv)
    buf_ref[0:hist, :] = buf_ref[rows:rows + hist, :]

    t_io = lax.broadcasted_iota(jnp.int32, (CHUNK, LANES), 0)
    s_io = lax.broadcasted_iota(jnp.int32, (CHUNK, LANES), 1)
    x_dt = dtbuf_ref[...] + dtb_ref[...]
    dt = jnp.maximum(x_dt, 0.0) + jnp.log1p(jnp.exp(-jnp.abs(x_dt)))
    live = (t_io < rows) & (c * CHUNK + t_io >= first_valid)
    dt = jnp.where(live, dt, 0.0)
    la = dt * a_ref[...]
    tril = (s_io <= t_io).astype(BF16)
    cum = sum(jnp.dot(tril, part, preferred_element_type=F32) for part in _split3(la))
    cum_t = cum.T
    causal = s_io <= t_io
    lane_lo = s_io < SSD_HEAD_DIM
    row_lo = t_io < SSD_HEAD_DIM

    def col(x, h):
        return jnp.broadcast_to(x[:, h:h + 1], (CHUNK, LANES))

    def pair_cols(x, h0):
        return jnp.where(lane_lo, col(x, h0), col(x, h0 + 1))

    heads_per_group = SSD_HEADS // SSD_GROUPS
    for g in range(SSD_GROUPS):
        b_g = xc_ref[:, SSD_INNER + D_STATE * g:SSD_INNER + D_STATE * (g + 1)].astype(BF16)
        c_g = xc_ref[:, SSD_INNER + GROUP_BC + D_STATE * g:
                     SSD_INNER + GROUP_BC + D_STATE * (g + 1)].astype(BF16)
        cb = lax.dot_general(c_g, b_g, (((1,), (1,)), ((), ())), preferred_element_type=F32)
        for pr in range(heads_per_group // 2):
            h0 = heads_per_group * g + 2 * pr
            lanes = slice(SSD_HEAD_DIM * h0, SSD_HEAD_DIM * (h0 + 2))
            xdt = xc_ref[:, lanes] * pair_cols(dt, h0)
            cum_p = pair_cols(cum, h0)
            y = None
            for hx, keep in ((h0, lane_lo), (h0 + 1, ~lane_lo)):
                seg = col(cum, hx) - jnp.broadcast_to(cum_t[hx:hx + 1, :], (CHUNK, LANES))
                decay = jnp.where(causal, jnp.exp(jnp.where(causal, seg, 0.0)), 0.0)
                m = (cb * decay).astype(BF16)
                xm = jnp.where(keep, xdt, 0.0).astype(BF16)
                d = jnp.dot(m, xm, preferred_element_type=F32)
                y = d if y is None else y + d
            h_pair = h_ref[lanes, :]
            y_off = lax.dot_general(c_g, h_pair.astype(BF16), (((1,), (1,)), ((), ())),
                                    preferred_element_type=F32)
            ybuf_ref[:, lanes] = y + y_off * jnp.exp(cum_p)
            cum_end = cum_p[CHUNK - 1:CHUNK, :]
            x_end = (xdt * jnp.exp(cum_end - cum_p)).T.astype(BF16)
            upd = jnp.dot(x_end, b_g, preferred_element_type=F32)
            end0 = jnp.broadcast_to(cum[CHUNK - 1:CHUNK, h0:h0 + 1], (CHUNK, LANES))
            end1 = jnp.broadcast_to(cum[CHUNK - 1:CHUNK, h0 + 1:h0 + 2], (CHUNK, LANES))
            h_ref[lanes, :] = jnp.exp(jnp.where(row_lo, end0, end1)) * h_pair + upd

    zz = z_ref[0]
    yv = (ybuf_ref[0:rows, :] + xc_ref[0:rows, 0:SSD_INNER] * dskip_ref[...]) * (zz * jax.nn.sigmoid(zz))
    gw = SSD_INNER // SSD_GROUPS
    outs = []
    for g in range(SSD_GROUPS):
        yg = yv[:, gw * g:gw * (g + 1)]
        outs.append(yg * _rms_scale(yg))
    y_ref[0] = (jnp.concatenate(outs, axis=1) * nw_ref[...]).astype(BF16)

    @pl.when(c == pl.num_programs(1) - 1)
    def _():
        hout_ref[0] = h_ref[...]


def _ssd(xbc, z, dt_raw, h0, hist, conv_w, conv_b, dt_bias, a_neg, d_skip, norm_w, *, rows, first_valid):
    b, l, _ = xbc.shape
    hp = SSD_HEADS * SSD_HEAD_DIM
    blk = lambda bi, ci: (bi, ci, 0)
    per_b = lambda bi, ci: (bi, 0, 0)
    const = lambda bi, ci: (0, 0)
    return pl.pallas_call(
        functools.partial(_ssd_kernel, rows=rows, first_valid=first_valid),
        grid=(b, l // rows),
        in_specs=[
            pl.BlockSpec((1, rows, CONV_DIM), blk),
            pl.BlockSpec((1, rows, SSD_INNER), blk),
            pl.BlockSpec((1, rows, LANES), blk),
            pl.BlockSpec((1, hp, D_STATE), per_b),
            pl.BlockSpec((1, SUBLANES, CONV_DIM), per_b),
            pl.BlockSpec((CONV_W, CONV_DIM), const),
            pl.BlockSpec((1, CONV_DIM), const),
            pl.BlockSpec((1, LANES), const),
            pl.BlockSpec((1, LANES), const),
            pl.BlockSpec((1, SSD_INNER), const),
            pl.BlockSpec((1, SSD_INNER), const),
        ],
        out_specs=(pl.BlockSpec((1, rows, SSD_INNER), blk), pl.BlockSpec((1, hp, D_STATE), per_b)),
        out_shape=(jax.ShapeDtypeStruct((b, l, SSD_INNER), BF16),
                   jax.ShapeDtypeStruct((b, hp, D_STATE), F32)),
        scratch_shapes=[
            pltpu.VMEM((SUBLANES + CHUNK, CONV_DIM), F32),
            pltpu.VMEM((CHUNK, CONV_DIM), F32),
            pltpu.VMEM((CHUNK, LANES), F32),
            pltpu.VMEM((CHUNK, SSD_INNER), F32),
            pltpu.VMEM((hp, D_STATE), F32),
        ],
        compiler_params=_cparams(("arbitrary", "arbitrary")),
        name="ssd",
    )(xbc, z, dt_raw, h0, hist, conv_w, conv_b, dt_bias, a_neg, d_skip, norm_w)


def _outproj_kernel(x_ref, attn_ref, ssd_ref, anw_ref, woa_ref, wos_ref, fnw_ref,
                    rwh_ref, rwl_ref, rb_ref, x2_ref, xn_ref, comb_ref):
    a = attn_ref[...]
    an = ((a * _rms_scale(a)) * anw_ref[...]).astype(BF16)
    d = jnp.dot(an, woa_ref[...], preferred_element_type=F32)
    d = d + jnp.dot(ssd_ref[...], wos_ref[...], preferred_element_type=F32)
    x2 = x_ref[...] + d
    x2_ref[...] = x2
    xn = (x2 * _rms_scale(x2)) * fnw_ref[...]
    xh = xn.astype(BF16)
    xl = (xn - xh.astype(F32)).astype(BF16)
    xn_ref[...] = xh

    logits = (jnp.dot(xh, rwh_ref[...], preferred_element_type=F32)
              + jnp.dot(xh, rwl_ref[...], preferred_element_type=F32)
              + jnp.dot(xl, rwh_ref[...], preferred_element_type=F32)) + rb_ref[...]
    lane = lax.broadcasted_iota(jnp.int32, logits.shape, 1)
    lane_f = lane.astype(F32)
    ninf = jnp.float32(-jnp.inf)
    far = jnp.float32(2 * LANES)

    def first_max(vals):
        m = jnp.max(vals, axis=1, keepdims=True)
        idx = jnp.min(jnp.where(vals == m, lane_f, far), axis=1, keepdims=True)
        return m, idx

    g_mask = (lane >= N_EXPERTS) & (lane < N_EXPERTS + N_EXPERT_GROUPS)
    gl = jnp.where(g_mask, logits, ninf)
    g_max, g_idx = first_max(gl)
    g_w = 1.0 / jnp.sum(jnp.where(g_mask, jnp.exp(gl - g_max), 0.0), axis=1, keepdims=True)
    g_sel = g_idx - N_EXPERTS
    e_mask = (lane < N_EXPERTS) & ((lane // EXPERTS_PER_GROUP).astype(F32) == g_sel)
    el = jnp.where(e_mask, logits, ninf)
    m1, i1 = first_max(el)
    m2, i2 = first_max(jnp.where(lane_f == i1, ninf, el))
    ex = jnp.exp(m2 - m1)
    w1 = g_w / (1.0 + ex)
    w2 = g_w * (ex / (1.0 + ex))
    comb_ref[...] = jnp.where(lane_f == i1, w1, jnp.where(lane_f == i2, w2, 0.0))


def _out_proj(x, attn, ssd, attn_norm_w, wo_a, wo_s, ffn_norm_w, rw_hi, rw_lo, rb, *, tm, row0):
    n = attn.shape[0] - row0 * tm
    rows_in = lambda i: (i + row0, 0)
    rows_out = lambda i: (i, 0)
    const = lambda i: (0, 0)
    return pl.pallas_call(
        _outproj_kernel,
        grid=(n // tm,),
        in_specs=[
            pl.BlockSpec((tm, D_MODEL), rows_in),
            pl.BlockSpec((tm, ATTN_WIDTH), rows_in),
            pl.BlockSpec((tm, SSD_INNER), rows_in),
            pl.BlockSpec((1, ATTN_WIDTH), const),
            pl.BlockSpec((ATTN_WIDTH, D_MODEL), const),
            pl.BlockSpec((SSD_INNER, D_MODEL), const),
            pl.BlockSpec((1, D_MODEL), const),
            pl.BlockSpec((D_MODEL, LANES), const),
            pl.BlockSpec((D_MODEL, LANES), const),
            pl.BlockSpec((1, LANES), const),
        ],
        out_specs=(pl.BlockSpec((tm, D_MODEL), rows_out), pl.BlockSpec((tm, D_MODEL), rows_out),
                   pl.BlockSpec((tm, LANES), rows_out)),
        out_shape=(jax.ShapeDtypeStruct((n, D_MODEL), F32), jax.ShapeDtypeStruct((n, D_MODEL), BF16),
                   jax.ShapeDtypeStruct((n, LANES), F32)),
        compiler_params=_cparams(("arbitrary",)),
        name="out_proj_router",
    )(x, attn, ssd, attn_norm_w, wo_a, wo_s, ffn_norm_w, rw_hi, rw_lo, rb)


def _moe_kernel(xn_ref, comb_ref, x2_ref, wg_ref, wu_ref, wd_ref, fw_ref, y_ref, acc_ref):
    e = pl.program_id(1)

    @pl.when(e == 0)
    def _():
        acc_ref[...] = jnp.zeros_like(acc_ref)

    x = xn_ref[...]
    g = jnp.dot(x, wg_ref[0], preferred_element_type=F32)
    u = jnp.dot(x, wu_ref[0], preferred_element_type=F32)
    comb = comb_ref[...]
    lane = lax.broadcasted_iota(jnp.int32, comb.shape, 1)
    c = jnp.sum(jnp.where(lane == e, comb, 0.0), axis=1, keepdims=True)
    h = ((g * jax.nn.sigmoid(g)) * u * c).astype(BF16)
    acc_ref[...] += jnp.dot(h, wd_ref[0], preferred_element_type=F32)

    @pl.when(e == pl.num_programs(1) - 1)
    def _():
        y = x2_ref[...] + acc_ref[...]
        y_ref[...] = (y * _rms_scale(y)) * fw_ref[...]


def _moe(xn, comb, x2, w_gate, w_up, w_down, final_w, *, tm):
    n = xn.shape[0]
    rows = lambda i, e: (i, 0)
    per_e = lambda i, e: (e, 0, 0)
    return pl.pallas_call(
        _moe_kernel,
        grid=(n // tm, N_EXPERTS),
        in_specs=[
            pl.BlockSpec((tm, D_MODEL), rows),
            pl.BlockSpec((tm, LANES), rows),
            pl.BlockSpec((tm, D_MODEL), rows),
            pl.BlockSpec((1, D_MODEL, D_EXPERT), per_e),
            pl.BlockSpec((1, D_MODEL, D_EXPERT), per_e),
            pl.BlockSpec((1, D_EXPERT, D_MODEL), per_e),
            pl.BlockSpec((1, D_MODEL), lambda i, e: (0, 0)),
        ],
        out_specs=pl.BlockSpec((tm, D_MODEL), rows),
        out_shape=jax.ShapeDtypeStruct((n, D_MODEL), F32),
        scratch_shapes=[pltpu.VMEM((tm, D_MODEL), F32)],
        compiler_params=_cparams(("arbitrary", "arbitrary")),
        name="moe",
    )(xn, comb, x2, w_gate, w_up, w_down, final_w)


def _largest_tile(n, cap):
    best = LANES
    for m in range(LANES, cap + 1, LANES):
        if n % m == 0:
            best = m
    return best


def _permute_key_tiles(x):
    t = x.shape[0]
    x = x.reshape(t // KEY_TILE, SUBLANES, STREAM, HEAD_PAIRS, 2 * ATTN_HEAD_DIM)
    return x.transpose(3, 0, 2, 1, 4).reshape(HEAD_PAIRS, t // KEY_TILE, KEY_TILE, 2 * ATTN_HEAD_DIM)


def kernel(x_prompt, x_sample, cache_k, cache_v, page_table, state_ssm, state_conv, meta_tokens,
           norm_mix_w, w_in, attn_logit_bias, conv_w, conv_b, dt_bias, a_log, d_skip, ssd_norm_w,
           attn_norm_w, w_out, norm_ffn_w, router_group_w, router_group_b, router_expert_w,
           router_expert_b, expert_w_gate, expert_w_up, expert_w_down, final_norm_w):
    assert w_in.shape[0] == 1 and x_prompt.shape[0] == 1
    seq = x_prompt.shape[1]
    dec_batch, dec_seq, _ = x_sample.shape
    assert seq % Q_BLOCK == 0 and dec_seq == SUBLANES
    n_pool = cache_k.shape[1]

    w_in0 = w_in[0]
    n_main = 3 * ATTN_WIDTH + SSD_INNER + CONV_DIM
    w_main = w_in0[:, :n_main].astype(BF16)
    w_dt = jnp.pad(w_in0[:, n_main:], ((0, 0), (0, LANES - SSD_HEADS))).astype(BF16)
    nmw = norm_mix_w[0][None]
    pad_heads = lambda v: jnp.pad(v, (0, LANES - SSD_HEADS))[None]
    dtb = pad_heads(dt_bias[0])
    a_neg = pad_heads(-jnp.exp(a_log[0]))
    dskip = jnp.repeat(d_skip[0], SSD_HEAD_DIM)[None]
    ssd_nw = ssd_norm_w[0][None]
    conv_w0, conv_b0 = conv_w[0], conv_b[0][None]
    wo = w_out[0].astype(BF16)
    wo_a, wo_s = wo[:ATTN_WIDTH], wo[ATTN_WIDTH:]
    rw = jnp.concatenate([router_expert_w[0], router_group_w[0]], axis=1)
    rw = jnp.pad(rw, ((0, 0), (0, LANES - rw.shape[1])))
    rw_hi = rw.astype(BF16)
    rw_lo = (rw - rw_hi.astype(F32)).astype(BF16)
    rb = jnp.concatenate([router_expert_b[0], router_group_b[0]])
    rb = jnp.pad(rb, (0, LANES - rb.shape[0]))[None]
    wg, wu, wd = (w[0].astype(BF16) for w in (expert_w_gate, expert_w_up, expert_w_down))
    anw, fnw, finw = attn_norm_w[0][None], norm_ffn_w[0][None], final_norm_w[None]
    bias = attn_logit_bias[0]

    def tail(x, attn, ssd, tm, row0):
        x2, xn, comb = _out_proj(x, attn, ssd, anw, wo_a, wo_s, fnw, rw_hi, rw_lo, rb, tm=tm, row0=row0)
        return _moe(xn, comb, x2, wg, wu, wd, finw, tm=_largest_tile(xn.shape[0], 512))

    xp = jnp.concatenate([jnp.zeros((PROMPT_PAD, D_MODEL), F32), meta_tokens.astype(F32), x_prompt[0]], axis=0)
    tp = xp.shape[0]
    q_p, k32_p, v32_p, kb_p, vb_p, z_p, xbc_p, dt_p = _in_proj(xp, nmw, w_main, w_dt, _largest_tile(tp, 640))
    k_perm = _permute_key_tiles(kb_p.reshape(tp, HEAD_PAIRS, 2 * ATTN_HEAD_DIM))
    v_t = _permute_key_tiles(vb_p.reshape(tp, HEAD_PAIRS, 2 * ATTN_HEAD_DIM)).transpose(0, 1, 3, 2)
    attn_p = _prompt_attention(bias, q_p.T, k_perm, v_t)
    hp = SSD_HEADS * SSD_HEAD_DIM
    ssd_p, h_p = _ssd(xbc_p[None], z_p[None], dt_p[None],
                      jnp.zeros((1, hp, D_STATE), F32), jnp.zeros((1, SUBLANES, CONV_DIM), F32),
                      conv_w0, conv_b0, dtb, a_neg, dskip, ssd_nw, rows=CHUNK, first_valid=PROMPT_PAD)
    y_prompt = tail(xp, attn_p, ssd_p[0], Q_BLOCK, 1)

    n_s = dec_batch * dec_seq
    xs = x_sample.reshape(n_s, D_MODEL)
    q_s, k32_s, v32_s, _, _, z_s, xbc_s, dt_s = _in_proj(xs, nmw, w_main, w_dt, _largest_tile(n_s, 512))
    q4 = q_s.reshape(dec_batch, dec_seq, ATTN_HEADS, ATTN_HEAD_DIM)
    eye = jnp.eye(ATTN_HEADS, dtype=BF16)
    q_bd = (q4.transpose(0, 2, 3, 1)[:, :, :, None, :] * eye[None, :, None, :, None])
    q_bd = q_bd.reshape(dec_batch, ATTN_WIDTH, ATTN_HEADS * dec_seq)
    bias_lanes = jnp.repeat(bias, dec_seq)[None]

    def spread(x):
        x = x.reshape(dec_batch, dec_seq, ATTN_WIDTH)
        return jnp.pad(x, ((0, 0), (0, KEY_TILE - dec_seq), (0, 0)))

    attn_s = _sample_attention(page_table, q_bd, bias_lanes, spread(k32_s), spread(v32_s),
                               cache_k[0].reshape(n_pool, PAGE_SIZE, ATTN_WIDTH),
                               cache_v[0].reshape(n_pool, PAGE_SIZE, ATTN_WIDTH), dec_seq)
    hist_s = jnp.pad(state_conv[0], ((0, 0), (SUBLANES - (CONV_W - 1), 0), (0, 0)))
    ssd_s, h_s = _ssd(xbc_s.reshape(dec_batch, dec_seq, CONV_DIM), z_s.reshape(dec_batch, dec_seq, SSD_INNER),
                      dt_s.reshape(dec_batch, dec_seq, LANES),
                      state_ssm[0].reshape(dec_batch, hp, D_STATE), hist_s,
                      conv_w0, conv_b0, dtb, a_neg, dskip, ssd_nw, rows=dec_seq, first_valid=0)
    y_sample = tail(xs, attn_s.reshape(n_s, ATTN_WIDTH), ssd_s.reshape(n_s, SSD_INNER), _largest_tile(n_s, 256), 0)

    heads = (ATTN_HEADS, ATTN_HEAD_DIM)
    state = (SSD_HEADS, SSD_HEAD_DIM, D_STATE)
    keep = slice(CONV_W - 1)
    xbc_s3 = xbc_s.reshape(dec_batch, dec_seq, CONV_DIM)
    return (
        y_prompt[None],
        y_sample.reshape(dec_batch, dec_seq, D_MODEL),
        k32_p[PROMPT_PAD:].reshape(1, 1, tp - PROMPT_PAD, *heads),
        v32_p[PROMPT_PAD:].reshape(1, 1, tp - PROMPT_PAD, *heads),
        h_p.reshape(1, 1, *state).astype(state_ssm.dtype),
        xbc_p[tp - (CONV_W - 1):][None, None],
        k32_s.reshape(1, dec_batch, dec_seq, *heads),
        v32_s.reshape(1, dec_batch, dec_seq, *heads),
        h_s.reshape(1, dec_batch, *state).astype(state_ssm.dtype),
        xbc_s3[:, dec_seq - (CONV_W - 1):][None],
    )
```

```python
import functools

import jax
import jax.numpy as jnp
from jax import lax
from jax.experimental import pallas as pl
from jax.experimental.pallas import tpu as pltpu

F32 = jnp.float32
BF16 = jnp.bfloat16

LANES = 128
SUBLANES = 8
VMEM_LIMIT = 56 * 1024 * 1024

N_META = 16
D_MODEL = 1024
ATTN_HEADS = 16
ATTN_HEAD_DIM = 64
ATTN_WIDTH = ATTN_HEADS * ATTN_HEAD_DIM
ATTN_SCALE = ATTN_HEAD_DIM ** -0.5
HEAD_PAIRS = ATTN_HEADS // 2
SSD_INNER = 2 * D_MODEL
SSD_HEAD_DIM = 64
SSD_HEADS = SSD_INNER // SSD_HEAD_DIM
SSD_GROUPS = 4
D_STATE = 128
CONV_W = 4
GROUP_BC = SSD_GROUPS * D_STATE
CONV_DIM = SSD_INNER + 2 * GROUP_BC
N_EXPERT_GROUPS = 4
EXPERTS_PER_GROUP = 8
N_EXPERTS = N_EXPERT_GROUPS * EXPERTS_PER_GROUP
D_EXPERT = 512
PAGE_SIZE = 128
EPS = 1e-6

Q_BLOCK = 256
KEY_TILE = 128
STREAM = KEY_TILE // SUBLANES
CHUNK = 128
PROMPT_PAD = Q_BLOCK - N_META
PAGES_PER_STEP = 8


def _rms_scale(x):
    return lax.rsqrt(jnp.mean(x * x, axis=-1, keepdims=True) + EPS)


def _cparams(sem):
    return pltpu.CompilerParams(dimension_semantics=sem, vmem_limit_bytes=VMEM_LIMIT)


def _inproj_kernel(x_ref, nw_ref, w_ref, wdt_ref,
                   q_ref, k32_ref, v32_ref, kb_ref, vb_ref, z_ref, xbc_ref, dt_ref, xn_ref):
    j = pl.program_id(1)

    @pl.when(j == 0)
    def _():
        x = x_ref[...]
        xn_ref[...] = ((x * _rms_scale(x)) * nw_ref[...]).astype(BF16)

    res = jnp.dot(xn_ref[...], w_ref[...], preferred_element_type=F32)

    @pl.when(j == 0)
    def _():
        q_ref[...] = (res * ATTN_SCALE).astype(BF16)

    @pl.when(j == 1)
    def _():
        k32_ref[...] = res
        kb_ref[...] = res.astype(BF16)

    @pl.when(j == 2)
    def _():
        v32_ref[...] = res
        vb_ref[...] = res.astype(BF16)

    @pl.when((j == 3) | (j == 4))
    def _():
        z_ref[...] = res

    @pl.when(j >= 5)
    def _():
        xbc_ref[...] = res

    @pl.when(j == 7)
    def _():
        dt_ref[...] = jnp.dot(xn_ref[...], wdt_ref[...], preferred_element_type=F32)


def _in_proj(x, norm_w, w_main, w_dt, tm):
    t = x.shape[0]
    nb = D_MODEL
    n_col = w_main.shape[1] // nb
    row = lambda i, j: (i, 0)
    out_shape = (
        jax.ShapeDtypeStruct((t, ATTN_WIDTH), BF16),
        jax.ShapeDtypeStruct((t, ATTN_WIDTH), F32),
        jax.ShapeDtypeStruct((t, ATTN_WIDTH), F32),
        jax.ShapeDtypeStruct((t, ATTN_WIDTH), BF16),
        jax.ShapeDtypeStruct((t, ATTN_WIDTH), BF16),
        jax.ShapeDtypeStruct((t, SSD_INNER), F32),
        jax.ShapeDtypeStruct((t, CONV_DIM), F32),
        jax.ShapeDtypeStruct((t, LANES), F32),
    )
    out_specs = (
        pl.BlockSpec((tm, nb), row), pl.BlockSpec((tm, nb), row), pl.BlockSpec((tm, nb), row),
        pl.BlockSpec((tm, nb), row), pl.BlockSpec((tm, nb), row),
        pl.BlockSpec((tm, nb), lambda i, j: (i, jnp.clip(j - 3, 0, 1))),
        pl.BlockSpec((tm, nb), lambda i, j: (i, jnp.clip(j - 5, 0, 2))),
        pl.BlockSpec((tm, LANES), row),
    )
    return pl.pallas_call(
        _inproj_kernel,
        grid=(t // tm, n_col),
        in_specs=[
            pl.BlockSpec((tm, D_MODEL), row),
            pl.BlockSpec((1, D_MODEL), lambda i, j: (0, 0)),
            pl.BlockSpec((D_MODEL, nb), lambda i, j: (0, j)),
            pl.BlockSpec((D_MODEL, LANES), lambda i, j: (0, 0)),
        ],
        out_specs=out_specs,
        out_shape=out_shape,
        scratch_shapes=[pltpu.VMEM((tm, D_MODEL), BF16)],
        compiler_params=_cparams(("arbitrary", "arbitrary")),
        name="in_proj",
    )(x, norm_w, w_main, w_dt)


def _key_in_tile(v, n):
    return STREAM * lax.broadcasted_iota(jnp.int32, (SUBLANES, n), 0) + v


def _sublane_suffix(tot):
    sub = lax.broadcasted_iota(jnp.int32, tot.shape, 0)
    s = tot
    for k in (1, 2, 4):
        rolled = pltpu.roll(s, SUBLANES - k, 0)
        s = s + jnp.where(sub < SUBLANES - k, rolled, 0.0)
    return s


def _softplus(z):
    sign = jnp.uint32(0x80000000)
    neg_abs = lax.bitcast_convert_type(lax.bitcast_convert_type(z, jnp.uint32) | sign, F32)
    return jnp.maximum(z, 0.0) + jnp.log(1.0 + jnp.exp(neg_abs))


def _sb_tile(logits, bias, carry, valid):
    nv = KEY_TILE // SUBLANES
    run = jnp.zeros_like(carry)
    part = [None] * nv
    for v in range(nv - 1, -1, -1):
        z = logits(v) + bias
        c = _softplus(z)
        if valid is not None:
            c = jnp.where(valid(v), c, 0.0)
        run = run + c
        part[v] = z - run
    incl = _sublane_suffix(run)
    base = (incl - run) + carry
    new_carry = carry + jnp.broadcast_to(incl[0:1], carry.shape)
    ws = []
    for v in range(nv):
        w = jnp.exp(part[v] - base)
        if valid is not None:
            w = jnp.where(valid(v), w, 0.0)
        ws.append(w)
    return jnp.concatenate(ws, axis=0), new_carry


def _prompt_attn_kernel(bias_ref, qt_ref, k_ref, vt_ref, o_ref, acc_ref, carry_ref,
                        s0_ref, s1_ref, w0_ref, w1_ref):
    p = pl.program_id(0)
    i = pl.program_id(1)
    n_kt = k_ref.shape[1]
    tiles_per_q = Q_BLOCK // KEY_TILE
    n_col = Q_BLOCK // LANES
    s_refs, w_refs = (s0_ref, s1_ref), (w0_ref, w1_ref)
    q2 = qt_ref[...]
    row = lax.broadcasted_iota(jnp.int32, q2.shape, 0)
    qpos = i * Q_BLOCK + lax.broadcasted_iota(jnp.int32, (SUBLANES, LANES), 1)
    qpads = [jnp.where((row // ATTN_HEAD_DIM) == hh, q2, jnp.zeros_like(q2)) for hh in range(2)]
    biases = [bias_ref[2 * p + hh] for hh in range(2)]
    hrows = [slice(ATTN_HEAD_DIM * hh, ATTN_HEAD_DIM * (hh + 1)) for hh in range(2)]

    def logits(kt, hh):
        return jnp.dot(k_ref[0, kt], qpads[hh], preferred_element_type=F32)

    def values(kt, hh):
        return jnp.dot(vt_ref[0, kt, hrows[hh], :], w_refs[hh][...], preferred_element_type=F32)

    def step(kt_prev, kt_cur, kt_next, masked):
        pv = [values(kt_prev, hh) for hh in range(2)]
        s_next = [logits(kt_next, hh) for hh in range(2)]
        ws = []
        for hh in range(2):
            halves = []
            for c in range(n_col):
                cols = slice(LANES * c, LANES * (c + 1))
                valid = None
                if masked:
                    def valid(v, c=c):
                        kpos = kt_cur * KEY_TILE + _key_in_tile(v, LANES)
                        return (kpos < qpos + LANES * c) & (kpos >= PROMPT_PAD)
                w, nc = _sb_tile(
                    lambda v, hh=hh, cols=cols: s_refs[hh][SUBLANES * v:SUBLANES * (v + 1), cols],
                    biases[hh], carry_ref[hh, :, cols], valid)
                carry_ref[hh, :, cols] = nc
                halves.append(w)
            ws.append(jnp.concatenate(halves, axis=1).astype(BF16))
        for hh in range(2):
            w_refs[hh][...] = ws[hh]
            s_refs[hh][...] = s_next[hh]
            acc_ref[hrows[hh], :] += pv[hh]

    acc_ref[...] = jnp.zeros_like(acc_ref)
    carry_ref[...] = jnp.zeros_like(carry_ref)
    newest = jnp.where(i > 0, tiles_per_q * i + tiles_per_q - 1, 1)
    for hh in range(2):
        s_refs[hh][...] = logits(newest, hh)
        w_refs[hh][...] = jnp.zeros_like(w_refs[hh])

    @pl.when(i > 0)
    def _():
        def diag(n, c):
            cur = tiles_per_q * i + (tiles_per_q - 1) - n
            step(jnp.minimum(cur + 1, n_kt - 1), cur, cur - 1, True)
            return c
        lax.fori_loop(0, tiles_per_q, diag, 0)

        def full(n, c):
            cur = tiles_per_q * i - 1 - n
            step(cur + 1, cur, cur - 1, False)
            return c
        lax.fori_loop(0, tiles_per_q * (i - 1), full, 0)

    step(jnp.where(i > 0, 2, 1), 1, 0, True)
    for hh in range(2):
        acc_ref[hrows[hh], :] += values(1, hh)
    o_ref[...] = acc_ref[...].T


def _prompt_attention(bias, q_t, k_perm, v_t):
    t = q_t.shape[1]
    n_kt = t // KEY_TILE
    pair_w = 2 * ATTN_HEAD_DIM
    return pl.pallas_call(
        _prompt_attn_kernel,
        grid=(HEAD_PAIRS, t // Q_BLOCK),
        in_specs=[
            pl.BlockSpec(memory_space=pltpu.SMEM),
            pl.BlockSpec((pair_w, Q_BLOCK), lambda p, i: (p, i)),
            pl.BlockSpec((1, n_kt, KEY_TILE, pair_w), lambda p, i: (p, 0, 0, 0)),
            pl.BlockSpec((1, n_kt, pair_w, KEY_TILE), lambda p, i: (p, 0, 0, 0)),
        ],
        out_specs=pl.BlockSpec((Q_BLOCK, pair_w), lambda p, i: (i, p)),
        out_shape=jax.ShapeDtypeStruct((t, ATTN_WIDTH), F32),
        scratch_shapes=[pltpu.VMEM((pair_w, Q_BLOCK), F32), pltpu.VMEM((2, SUBLANES, Q_BLOCK), F32),
                        pltpu.VMEM((KEY_TILE, Q_BLOCK), F32), pltpu.VMEM((KEY_TILE, Q_BLOCK), F32),
                        pltpu.VMEM((KEY_TILE, Q_BLOCK), BF16), pltpu.VMEM((KEY_TILE, Q_BLOCK), BF16)],
        compiler_params=_cparams(("arbitrary", "arbitrary")),
        name="prompt_attention",
    )(bias, q_t, k_perm, v_t)


def _sample_attn_kernel(pt_ref, qbd_ref, bias_ref, knew_ref, vnew_ref, *rest, dec_seq):
    n_pg = PAGES_PER_STEP
    k_refs = rest[:n_pg]
    v_refs = rest[n_pg:2 * n_pg]
    o_ref, acc_ref, carry_ref = rest[2 * n_pg:]
    s_id = pl.program_id(1)
    qbd = qbd_ref[0]
    bias = bias_ref[...]
    r_io = lax.broadcasted_iota(jnp.int32, (LANES, PAGE_SIZE), 0)
    c_io = lax.broadcasted_iota(jnp.int32, (LANES, PAGE_SIZE), 1)
    later_or_same = (r_io >= c_io).astype(BF16)

    def rows_of(ref):
        return ref[...].reshape(ATTN_WIDTH, PAGE_SIZE).astype(BF16)

    def tile(k_ref, v_ref, valid):
        z = jnp.dot(qbd, rows_of(k_ref), preferred_element_type=F32) + bias
        c = _softplus(z)
        if valid is not None:
            c = jnp.where(valid, c, 0.0)
        incl = sum(jnp.dot(part, later_or_same, preferred_element_type=F32) for part in _split3(c))
        carry = carry_ref[...]
        w = jnp.exp(z - (incl + carry))
        if valid is not None:
            w = jnp.where(valid, w, 0.0)
        carry_ref[...] = carry + jnp.broadcast_to(incl[:, 0:1], carry.shape)
        acc_ref[...] += jnp.dot(rows_of(v_ref), w.T.astype(BF16), preferred_element_type=F32)

    @pl.when(s_id == 0)
    def _():
        acc_ref[...] = jnp.zeros_like(acc_ref)
        carry_ref[...] = jnp.zeros_like(carry_ref)
        tile(knew_ref, vnew_ref, c_io < r_io % dec_seq)

    for r in range(n_pg):
        tile(k_refs[r], v_refs[r], None)

    @pl.when(s_id == pl.num_programs(1) - 1)
    def _():
        row = lax.broadcasted_iota(jnp.int32, acc_ref.shape, 0)
        lane = lax.broadcasted_iota(jnp.int32, acc_ref.shape, 1)
        own = jnp.where(row // ATTN_HEAD_DIM == lane // dec_seq, acc_ref[...], 0.0)
        fold = (r_io % dec_seq == c_io).astype(BF16)
        out_t = sum(jnp.dot(part, fold, preferred_element_type=F32) for part in _split3(own))
        o_ref[0] = out_t.T[0:dec_seq, :]


def _sample_attention(page_table, q_bd, bias_rows, k_new_t, v_new_t, cache_k_t, cache_v_t, dec_seq):
    b, n_pages = page_table.shape
    n_steps = n_pages // PAGES_PER_STEP
    page_block = (None, ATTN_HEADS, ATTN_HEAD_DIM, PAGE_SIZE)

    def page_spec(r):
        def idx(bi, si, pt):
            return (pt[bi, n_pages - 1 - (si * PAGES_PER_STEP + r)], 0, 0, 0)
        return pl.BlockSpec(page_block, idx)

    per_seq = lambda bi, si, pt: (bi, 0, 0)
    per_seq4 = lambda bi, si, pt: (bi, 0, 0, 0)
    grid_spec = pltpu.PrefetchScalarGridSpec(
        num_scalar_prefetch=1,
        grid=(b, n_steps),
        in_specs=[
            pl.BlockSpec((1, LANES, ATTN_WIDTH), per_seq),
            pl.BlockSpec((LANES, PAGE_SIZE), lambda bi, si, pt: (0, 0)),
            pl.BlockSpec(page_block, per_seq4),
            pl.BlockSpec(page_block, per_seq4),
        ] + [page_spec(r) for r in range(PAGES_PER_STEP)] * 2,
        out_specs=pl.BlockSpec((1, dec_seq, ATTN_WIDTH), per_seq),
        scratch_shapes=[pltpu.VMEM((ATTN_WIDTH, LANES), F32), pltpu.VMEM((LANES, PAGE_SIZE), F32)],
    )
    return pl.pallas_call(
        functools.partial(_sample_attn_kernel, dec_seq=dec_seq),
        grid_spec=grid_spec,
        out_shape=jax.ShapeDtypeStruct((b, dec_seq, ATTN_WIDTH), F32),
        compiler_params=_cparams(("arbitrary", "arbitrary")),
        name="sample_attention",
    )(page_table, q_bd, bias_rows, k_new_t, v_new_t,
      *([cache_k_t] * PAGES_PER_STEP), *([cache_v_t] * PAGES_PER_STEP))


def _split3(x):
    hi = x.astype(BF16)
    r1 = x - hi.astype(F32)
    mid = r1.astype(BF16)
    lo = (r1 - mid.astype(F32)).astype(BF16)
    return hi, mid, lo


def _ssd_kernel(xbc_ref, z_ref, dt_ref, h0_ref, hist_ref, convw_ref, convb_ref, dtb_ref, a_ref,
                dskip_ref, nw_ref, y_ref, hout_ref, buf_ref, xc_ref, dtbuf_ref, ybuf_ref, h_ref,
                *, rows, first_valid):
    c = pl.program_id(1)
    hist = SUBLANES

    @pl.when(c == 0)
    def _():
        h_ref[...] = h0_ref[0]
        buf_ref[...] = jnp.zeros_like(buf_ref)
        buf_ref[0:hist, :] = hist_ref[0]
        dtbuf_ref[...] = jnp.zeros_like(dtbuf_ref)

    buf_ref[hist:hist + rows, :] = xbc_ref[0]
    dtbuf_ref[0:rows, :] = dt_ref[0]

    conv = convb_ref[...]
    for tap in range(CONV_W):
        start = hist - (CONV_W - 1) + tap
        conv = conv + buf_ref[start:start + CHUNK, :] * convw_ref[tap:tap + 1, :]
    xc_ref[...] = conv * jax.nn.sigmoid(conv)
    buf_ref[0:hist, :] = buf_ref[rows:rows + hist, :]

    t_io = lax.broadcasted_iota(jnp.int32, (CHUNK, LANES), 0)
    s_io = lax.broadcasted_iota(jnp.int32, (CHUNK, LANES), 1)
    x_dt = dtbuf_ref[...] + dtb_ref[...]
    dt = jnp.maximum(x_dt, 0.0) + jnp.log1p(jnp.exp(-jnp.abs(x_dt)))
    live = (t_io < rows) & (c * CHUNK + t_io >= first_valid)
    dt = jnp.where(live, dt, 0.0)
    la = dt * a_ref[...]
    tril = (s_io <= t_io).astype(BF16)
    cum = sum(jnp.dot(tril, part, preferred_element_type=F32) for part in _split3(la))
    cum_t = cum.T
    causal = s_io <= t_io
    lane_lo = s_io < SSD_HEAD_DIM
    row_lo = t_io < SSD_HEAD_DIM

    def col(x, h):
        return jnp.broadcast_to(x[:, h:h + 1], (CHUNK, LANES))

    def pair_cols(x, h0):
        return jnp.where(lane_lo, col(x, h0), col(x, h0 + 1))

    heads_per_group = SSD_HEADS // SSD_GROUPS
    for g in range(SSD_GROUPS):
        b_g = xc_ref[:, SSD_INNER + D_STATE * g:SSD_INNER + D_STATE * (g + 1)].astype(BF16)
        c_g = xc_ref[:, SSD_INNER + GROUP_BC + D_STATE * g:
                     SSD_INNER + GROUP_BC + D_STATE * (g + 1)].astype(BF16)
        cb = lax.dot_general(c_g, b_g, (((1,), (1,)), ((), ())), preferred_element_type=F32)
        for pr in range(heads_per_group // 2):
            h0 = heads_per_group * g + 2 * pr
            lanes = slice(SSD_HEAD_DIM * h0, SSD_HEAD_DIM * (h0 + 2))
            xdt = xc_ref[:, lanes] * pair_cols(dt, h0)
            cum_p = pair_cols(cum, h0)
            y = None
            for hx, keep in ((h0, lane_lo), (h0 + 1, ~lane_lo)):
                seg = col(cum, hx) - jnp.broadcast_to(cum_t[hx:hx + 1, :], (CHUNK, LANES))
                decay = jnp.where(causal, jnp.exp(jnp.where(causal, seg, 0.0)), 0.0)
                m = (cb * decay).astype(BF16)
                xm = jnp.where(keep, xdt, 0.0).astype(BF16)
                d = jnp.dot(m, xm, preferred_element_type=F32)
                y = d if y is None else y + d
            h_pair = h_ref[lanes, :]
            y_off = lax.dot_general(c_g, h_pair.astype(BF16), (((1,), (1,)), ((), ())),
                                    preferred_element_type=F32)
            ybuf_ref[:, lanes] = y + y_off * jnp.exp(cum_p)
            cum_end = cum_p[CHUNK - 1:CHUNK, :]
            x_end = (xdt * jnp.exp(cum_end - cum_p)).T.astype(BF16)
            upd = jnp.dot(x_end, b_g, preferred_element_type=F32)
            end0 = jnp.broadcast_to(cum[CHUNK - 1:CHUNK, h0:h0 + 1], (CHUNK, LANES))
            end1 = jnp.broadcast_to(cum[CHUNK - 1:CHUNK, h0 + 1:h0 + 2], (CHUNK, LANES))
            h_ref[lanes, :] = jnp.exp(jnp.where(row_lo, end0, end1)) * h_pair + upd

    zz = z_ref[0]
    yv = (ybuf_ref[0:rows, :] + xc_ref[0:rows, 0:SSD_INNER] * dskip_ref[...]) * (zz * jax.nn.sigmoid(zz))
    gw = SSD_INNER // SSD_GROUPS
    outs = []
    for g in range(SSD_GROUPS):
        yg = yv[:, gw * g:gw * (g + 1)]
        outs.append(yg * _rms_scale(yg))
    y_ref[0] = (jnp.concatenate(outs, axis=1) * nw_ref[...]).astype(BF16)

    @pl.when(c == pl.num_programs(1) - 1)
    def _():
        hout_ref[0] = h_ref[...]


def _ssd(xbc, z, dt_raw, h0, hist, conv_w, conv_b, dt_bias, a_neg, d_skip, norm_w, *, rows, first_valid):
    b, l, _ = xbc.shape
    hp = SSD_HEADS * SSD_HEAD_DIM
    blk = lambda bi, ci: (bi, ci, 0)
    per_b = lambda bi, ci: (bi, 0, 0)
    const = lambda bi, ci: (0, 0)
    return pl.pallas_call(
        functools.partial(_ssd_kernel, rows=rows, first_valid=first_valid),
        grid=(b, l // rows),
        in_specs=[
            pl.BlockSpec((1, rows, CONV_DIM), blk),
            pl.BlockSpec((1, rows, SSD_INNER), blk),
            pl.BlockSpec((1, rows, LANES), blk),
            pl.BlockSpec((1, hp, D_STATE), per_b),
            pl.BlockSpec((1, SUBLANES, CONV_DIM), per_b),
            pl.BlockSpec((CONV_W, CONV_DIM), const),
            pl.BlockSpec((1, CONV_DIM), const),
            pl.BlockSpec((1, LANES), const),
            pl.BlockSpec((1, LANES), const),
            pl.BlockSpec((1, SSD_INNER), const),
            pl.BlockSpec((1, SSD_INNER), const),
        ],
        out_specs=(pl.BlockSpec((1, rows, SSD_INNER), blk), pl.BlockSpec((1, hp, D_STATE), per_b)),
        out_shape=(jax.ShapeDtypeStruct((b, l, SSD_INNER), BF16),
                   jax.ShapeDtypeStruct((b, hp, D_STATE), F32)),
        scratch_shapes=[
            pltpu.VMEM((SUBLANES + CHUNK, CONV_DIM), F32),
            pltpu.VMEM((CHUNK, CONV_DIM), F32),
            pltpu.VMEM((CHUNK, LANES), F32),
            pltpu.VMEM((CHUNK, SSD_INNER), F32),
            pltpu.VMEM((hp, D_STATE), F32),
        ],
        compiler_params=_cparams(("arbitrary", "arbitrary")),
        name="ssd",
    )(xbc, z, dt_raw, h0, hist, conv_w, conv_b, dt_bias, a_neg, d_skip, norm_w)


def _outproj_kernel(x_ref, attn_ref, ssd_ref, anw_ref, woa_ref, wos_ref, fnw_ref,
                    rwh_ref, rwl_ref, rb_ref, x2_ref, xn_ref, comb_ref):
    a = attn_ref[...]
    an = ((a * _rms_scale(a)) * anw_ref[...]).astype(BF16)
    d = jnp.dot(an, woa_ref[...], preferred_element_type=F32)
    d = d + jnp.dot(ssd_ref[...], wos_ref[...], preferred_element_type=F32)
    x2 = x_ref[...] + d
    x2_ref[...] = x2
    xn = (x2 * _rms_scale(x2)) * fnw_ref[...]
    xh = xn.astype(BF16)
    xl = (xn - xh.astype(F32)).astype(BF16)
    xn_ref[...] = xh

    logits = (jnp.dot(xh, rwh_ref[...], preferred_element_type=F32)
              + jnp.dot(xh, rwl_ref[...], preferred_element_type=F32)
              + jnp.dot(xl, rwh_ref[...], preferred_element_type=F32)) + rb_ref[...]
    lane = lax.broadcasted_iota(jnp.int32, logits.shape, 1)
    lane_f = lane.astype(F32)
    ninf = jnp.float32(-jnp.inf)
    far = jnp.float32(2 * LANES)

    def first_max(vals):
        m = jnp.max(vals, axis=1, keepdims=True)
        idx = jnp.min(jnp.where(vals == m, lane_f, far), axis=1, keepdims=True)
        return m, idx

    g_mask = (lane >= N_EXPERTS) & (lane < N_EXPERTS + N_EXPERT_GROUPS)
    gl = jnp.where(g_mask, logits, ninf)
    g_max, g_idx = first_max(gl)
    g_w = 1.0 / jnp.sum(jnp.where(g_mask, jnp.exp(gl - g_max), 0.0), axis=1, keepdims=True)
    g_sel = g_idx - N_EXPERTS
    e_mask = (lane < N_EXPERTS) & ((lane // EXPERTS_PER_GROUP).astype(F32) == g_sel)
    el = jnp.where(e_mask, logits, ninf)
    m1, i1 = first_max(el)
    m2, i2 = first_max(jnp.where(lane_f == i1, ninf, el))
    ex = jnp.exp(m2 - m1)
    w1 = g_w / (1.0 + ex)
    w2 = g_w * (ex / (1.0 + ex))
    comb_ref[...] = jnp.where(lane_f == i1, w1, jnp.where(lane_f == i2, w2, 0.0))


def _out_proj(x, attn, ssd, attn_norm_w, wo_a, wo_s, ffn_norm_w, rw_hi, rw_lo, rb, *, tm, row0):
    n = attn.shape[0] - row0 * tm
    rows_in = lambda i: (i + row0, 0)
    rows_out = lambda i: (i, 0)
    const = lambda i: (0, 0)
    return pl.pallas_call(
        _outproj_kernel,
        grid=(n // tm,),
        in_specs=[
            pl.BlockSpec((tm, D_MODEL), rows_in),
            pl.BlockSpec((tm, ATTN_WIDTH), rows_in),
            pl.BlockSpec((tm, SSD_INNER), rows_in),
            pl.BlockSpec((1, ATTN_WIDTH), const),
            pl.BlockSpec((ATTN_WIDTH, D_MODEL), const),
            pl.BlockSpec((SSD_INNER, D_MODEL), const),
            pl.BlockSpec((1, D_MODEL), const),
            pl.BlockSpec((D_MODEL, LANES), const),
            pl.BlockSpec((D_MODEL, LANES), const),
            pl.BlockSpec((1, LANES), const),
        ],
        out_specs=(pl.BlockSpec((tm, D_MODEL), rows_out), pl.BlockSpec((tm, D_MODEL), rows_out),
                   pl.BlockSpec((tm, LANES), rows_out)),
        out_shape=(jax.ShapeDtypeStruct((n, D_MODEL), F32), jax.ShapeDtypeStruct((n, D_MODEL), BF16),
                   jax.ShapeDtypeStruct((n, LANES), F32)),
        compiler_params=_cparams(("arbitrary",)),
        name="out_proj_router",
    )(x, attn, ssd, attn_norm_w, wo_a, wo_s, ffn_norm_w, rw_hi, rw_lo, rb)


def _moe_kernel(xn_ref, comb_ref, x2_ref, wg_ref, wu_ref, wd_ref, fw_ref, y_ref, acc_ref):
    e = pl.program_id(1)

    @pl.when(e == 0)
    def _():
        acc_ref[...] = jnp.zeros_like(acc_ref)

    x = xn_ref[...]
    g = jnp.dot(x, wg_ref[0], preferred_element_type=F32)
    u = jnp.dot(x, wu_ref[0], preferred_element_type=F32)
    comb = comb_ref[...]
    lane = lax.broadcasted_iota(jnp.int32, comb.shape, 1)
    c = jnp.sum(jnp.where(lane == e, comb, 0.0), axis=1, keepdims=True)
    h = ((g * jax.nn.sigmoid(g)) * u * c).astype(BF16)
    acc_ref[...] += jnp.dot(h, wd_ref[0], preferred_element_type=F32)

    @pl.when(e == pl.num_programs(1) - 1)
    def _():
        y = x2_ref[...] + acc_ref[...]
        y_ref[...] = (y * _rms_scale(y)) * fw_ref[...]


def _moe(xn, comb, x2, w_gate, w_up, w_down, final_w, *, tm):
    n = xn.shape[0]
    rows = lambda i, e: (i, 0)
    per_e = lambda i, e: (e, 0, 0)
    return pl.pallas_call(
        _moe_kernel,
        grid=(n // tm, N_EXPERTS),
        in_specs=[
            pl.BlockSpec((tm, D_MODEL), rows),
            pl.BlockSpec((tm, LANES), rows),
            pl.BlockSpec((tm, D_MODEL), rows),
            pl.BlockSpec((1, D_MODEL, D_EXPERT), per_e),
            pl.BlockSpec((1, D_MODEL, D_EXPERT), per_e),
            pl.BlockSpec((1, D_EXPERT, D_MODEL), per_e),
            pl.BlockSpec((1, D_MODEL), lambda i, e: (0, 0)),
        ],
        out_specs=pl.BlockSpec((tm, D_MODEL), rows),
        out_shape=jax.ShapeDtypeStruct((n, D_MODEL), F32),
        scratch_shapes=[pltpu.VMEM((tm, D_MODEL), F32)],
        compiler_params=_cparams(("arbitrary", "arbitrary")),
        name="moe",
    )(xn, comb, x2, w_gate, w_up, w_down, final_w)


def _largest_tile(n, cap):
    best = LANES
    for m in range(LANES, cap + 1, LANES):
        if n % m == 0:
            best = m
    return best


def _permute_key_tiles(x):
    t = x.shape[0]
    x = x.reshape(t // KEY_TILE, SUBLANES, STREAM, HEAD_PAIRS, 2 * ATTN_HEAD_DIM)
    return x.transpose(3, 0, 2, 1, 4).reshape(HEAD_PAIRS, t // KEY_TILE, KEY_TILE, 2 * ATTN_HEAD_DIM)


def kernel(x_prompt, x_sample, cache_k, cache_v, page_table, state_ssm, state_conv, meta_tokens,
           norm_mix_w, w_in, attn_logit_bias, conv_w, conv_b, dt_bias, a_log, d_skip, ssd_norm_w,
           attn_norm_w, w_out, norm_ffn_w, router_group_w, router_group_b, router_expert_w,
           router_expert_b, expert_w_gate, expert_w_up, expert_w_down, final_norm_w):
    assert w_in.shape[0] == 1 and x_prompt.shape[0] == 1
    seq = x_prompt.shape[1]
    dec_batch, dec_seq, _ = x_sample.shape
    assert seq % Q_BLOCK == 0 and dec_seq == SUBLANES
    n_pool = cache_k.shape[1]

    w_in0 = w_in[0]
    n_main = 3 * ATTN_WIDTH + SSD_INNER + CONV_DIM
    w_main = w_in0[:, :n_main].astype(BF16)
    w_dt = jnp.pad(w_in0[:, n_main:], ((0, 0), (0, LANES - SSD_HEADS))).astype(BF16)
    nmw = norm_mix_w[0][None]
    pad_heads = lambda v: jnp.pad(v, (0, LANES - SSD_HEADS))[None]
    dtb = pad_heads(dt_bias[0])
    a_neg = pad_heads(-jnp.exp(a_log[0]))
    dskip = jnp.repeat(d_skip[0], SSD_HEAD_DIM)[None]
    ssd_nw = ssd_norm_w[0][None]
    conv_w0, conv_b0 = conv_w[0], conv_b[0][None]
    wo = w_out[0].astype(BF16)
    wo_a, wo_s = wo[:ATTN_WIDTH], wo[ATTN_WIDTH:]
    rw = jnp.concatenate([router_expert_w[0], router_group_w[0]], axis=1)
    rw = jnp.pad(rw, ((0, 0), (0, LANES - rw.shape[1])))
    rw_hi = rw.astype(BF16)
    rw_lo = (rw - rw_hi.astype(F32)).astype(BF16)
    rb = jnp.concatenate([router_expert_b[0], router_group_b[0]])
    rb = jnp.pad(rb, (0, LANES - rb.shape[0]))[None]
    wg, wu, wd = (w[0].astype(BF16) for w in (expert_w_gate, expert_w_up, expert_w_down))
    anw, fnw, finw = attn_norm_w[0][None], norm_ffn_w[0][None], final_norm_w[None]
    bias = attn_logit_bias[0]

    def tail(x, attn, ssd, tm, row0):
        x2, xn, comb = _out_proj(x, attn, ssd, anw, wo_a, wo_s, fnw, rw_hi, rw_lo, rb, tm=tm, row0=row0)
        return _moe(xn, comb, x2, wg, wu, wd, finw, tm=_largest_tile(xn.shape[0], 512))

    xp = jnp.concatenate([jnp.zeros((PROMPT_PAD, D_MODEL), F32), meta_tokens.astype(F32), x_prompt[0]], axis=0)
    tp = xp.shape[0]
    q_p, k32_p, v32_p, kb_p, vb_p, z_p, xbc_p, dt_p = _in_proj(xp, nmw, w_main, w_dt, _largest_tile(tp, 640))
    k_perm = _permute_key_tiles(kb_p.reshape(tp, HEAD_PAIRS, 2 * ATTN_HEAD_DIM))
    v_t = _permute_key_tiles(vb_p.reshape(tp, HEAD_PAIRS, 2 * ATTN_HEAD_DIM)).transpose(0, 1, 3, 2)
    attn_p = _prompt_attention(bias, q_p.T, k_perm, v_t)
    hp = SSD_HEADS * SSD_HEAD_DIM
    ssd_p, h_p = _ssd(xbc_p[None], z_p[None], dt_p[None],
                      jnp.zeros((1, hp, D_STATE), F32), jnp.zeros((1, SUBLANES, CONV_DIM), F32),
                      conv_w0, conv_b0, dtb, a_neg, dskip, ssd_nw, rows=CHUNK, first_valid=PROMPT_PAD)
    y_prompt = tail(xp, attn_p, ssd_p[0], Q_BLOCK, 1)

    n_s = dec_batch * dec_seq
    xs = x_sample.reshape(n_s, D_MODEL)
    q_s, k32_s, v32_s, _, _, z_s, xbc_s, dt_s = _in_proj(xs, nmw, w_main, w_dt, _largest_tile(n_s, 512))
    q4 = q_s.reshape(dec_batch, dec_seq, ATTN_HEADS, ATTN_HEAD_DIM)
    eye = jnp.eye(ATTN_HEADS, dtype=BF16)
    q_bd = q4.transpose(0, 2, 1, 3)[:, :, :, None, :] * eye[None, :, None, :, None]
    q_bd = q_bd.reshape(dec_batch, ATTN_HEADS * dec_seq, ATTN_WIDTH)
    bias_rows = jnp.broadcast_to(jnp.repeat(bias, dec_seq)[:, None], (ATTN_HEADS * dec_seq, PAGE_SIZE))

    def pages_t(x):
        return jnp.moveaxis(x, -3, -1)

    def spread(x):
        x = pages_t(x.reshape(dec_batch, dec_seq, ATTN_HEADS, ATTN_HEAD_DIM))
        return jnp.pad(x, ((0, 0), (0, 0), (0, 0), (0, PAGE_SIZE - dec_seq)))

    attn_s = _sample_attention(page_table, q_bd, bias_rows, spread(k32_s), spread(v32_s),
                               pages_t(cache_k[0]), pages_t(cache_v[0]), dec_seq)
    hist_s = jnp.pad(state_conv[0], ((0, 0), (SUBLANES - (CONV_W - 1), 0), (0, 0)))
    ssd_s, h_s = _ssd(xbc_s.reshape(dec_batch, dec_seq, CONV_DIM), z_s.reshape(dec_batch, dec_seq, SSD_INNER),
                      dt_s.reshape(dec_batch, dec_seq, LANES),
                      state_ssm[0].reshape(dec_batch, hp, D_STATE), hist_s,
                      conv_w0, conv_b0, dtb, a_neg, dskip, ssd_nw, rows=dec_seq, first_valid=0)
    y_sample = tail(xs, attn_s.reshape(n_s, ATTN_WIDTH), ssd_s.reshape(n_s, SSD_INNER), _largest_tile(n_s, 256), 0)

    heads = (ATTN_HEADS, ATTN_HEAD_DIM)
    state = (SSD_HEADS, SSD_HEAD_DIM, D_STATE)
    keep = slice(CONV_W - 1)
    xbc_s3 = xbc_s.reshape(dec_batch, dec_seq, CONV_DIM)
    return (
        y_prompt[None],
        y_sample.reshape(dec_batch, dec_seq, D_MODEL),
        k32_p[PROMPT_PAD:].reshape(1, 1, tp - PROMPT_PAD, *heads),
        v32_p[PROMPT_PAD:].reshape(1, 1, tp - PROMPT_PAD, *heads),
        h_p.reshape(1, 1, *state).astype(state_ssm.dtype),
        xbc_p[tp - (CONV_W - 1):][None, None],
        k32_s.reshape(1, dec_batch, dec_seq, *heads),
        v32_s.reshape(1, dec_batch, dec_seq, *heads),
        h_s.reshape(1, dec_batch, *state).astype(state_ssm.dtype),
        xbc_s3[:, dec_seq - (CONV_W - 1):][None],
    )
```

```python
import functools

import jax
import jax.numpy as jnp
from jax import lax
from jax.experimental import pallas as pl
from jax.experimental.pallas import tpu as pltpu

F32 = jnp.float32
BF16 = jnp.bfloat16

LANES = 128
SUBLANES = 8
VMEM_LIMIT = 56 * 1024 * 1024

N_META = 16
D_MODEL = 1024
ATTN_HEADS = 16
ATTN_HEAD_DIM = 64
ATTN_WIDTH = ATTN_HEADS * ATTN_HEAD_DIM
ATTN_SCALE = ATTN_HEAD_DIM ** -0.5
HEAD_PAIRS = ATTN_HEADS // 2
SSD_INNER = 2 * D_MODEL
SSD_HEAD_DIM = 64
SSD_HEADS = SSD_INNER // SSD_HEAD_DIM
SSD_GROUPS = 4
D_STATE = 128
CONV_W = 4
GROUP_BC = SSD_GROUPS * D_STATE
CONV_DIM = SSD_INNER + 2 * GROUP_BC
N_EXPERT_GROUPS = 4
EXPERTS_PER_GROUP = 8
N_EXPERTS = N_EXPERT_GROUPS * EXPERTS_PER_GROUP
D_EXPERT = 512
PAGE_SIZE = 128
EPS = 1e-6

Q_BLOCK = 256
KEY_TILE = 128
STREAM = KEY_TILE // SUBLANES
TILES_PER_STEP = Q_BLOCK // KEY_TILE
CHUNK = 128
PROMPT_PAD = Q_BLOCK - N_META
PAGES_PER_STEP = 8


def _rms_scale(x):
    return lax.rsqrt(jnp.mean(x * x, axis=-1, keepdims=True) + EPS)


def _cparams(sem):
    return pltpu.CompilerParams(dimension_semantics=sem, vmem_limit_bytes=VMEM_LIMIT)


def _inproj_kernel(x_ref, nw_ref, w_ref, wdt_ref,
                   q_ref, k32_ref, v32_ref, kb_ref, vb_ref, z_ref, xbc_ref, dt_ref, xn_ref):
    j = pl.program_id(1)

    @pl.when(j == 0)
    def _():
        x = x_ref[...]
        xn_ref[...] = ((x * _rms_scale(x)) * nw_ref[...]).astype(BF16)

    res = jnp.dot(xn_ref[...], w_ref[...], preferred_element_type=F32)

    @pl.when(j == 0)
    def _():
        q_ref[...] = (res * ATTN_SCALE).astype(BF16)

    @pl.when(j == 1)
    def _():
        k32_ref[...] = res
        kb_ref[...] = res.astype(BF16)

    @pl.when(j == 2)
    def _():
        v32_ref[...] = res
        vb_ref[...] = res.astype(BF16)

    @pl.when((j == 3) | (j == 4))
    def _():
        z_ref[...] = res

    @pl.when(j >= 5)
    def _():
        xbc_ref[...] = res

    @pl.when(j == 7)
    def _():
        dt_ref[...] = jnp.dot(xn_ref[...], wdt_ref[...], preferred_element_type=F32)


def _in_proj(x, norm_w, w_main, w_dt, tm):
    t = x.shape[0]
    nb = D_MODEL
    n_col = w_main.shape[1] // nb
    row = lambda i, j: (i, 0)
    out_shape = (
        jax.ShapeDtypeStruct((t, ATTN_WIDTH), BF16),
        jax.ShapeDtypeStruct((t, ATTN_WIDTH), F32),
        jax.ShapeDtypeStruct((t, ATTN_WIDTH), F32),
        jax.ShapeDtypeStruct((t, ATTN_WIDTH), BF16),
        jax.ShapeDtypeStruct((t, ATTN_WIDTH), BF16),
        jax.ShapeDtypeStruct((t, SSD_INNER), F32),
        jax.ShapeDtypeStruct((t, CONV_DIM), F32),
        jax.ShapeDtypeStruct((t, LANES), F32),
    )
    out_specs = (
        pl.BlockSpec((tm, nb), row), pl.BlockSpec((tm, nb), row), pl.BlockSpec((tm, nb), row),
        pl.BlockSpec((tm, nb), row), pl.BlockSpec((tm, nb), row),
        pl.BlockSpec((tm, nb), lambda i, j: (i, jnp.clip(j - 3, 0, 1))),
        pl.BlockSpec((tm, nb), lambda i, j: (i, jnp.clip(j - 5, 0, 2))),
        pl.BlockSpec((tm, LANES), row),
    )
    return pl.pallas_call(
        _inproj_kernel,
        grid=(t // tm, n_col),
        in_specs=[
            pl.BlockSpec((tm, D_MODEL), row),
            pl.BlockSpec((1, D_MODEL), lambda i, j: (0, 0)),
            pl.BlockSpec((D_MODEL, nb), lambda i, j: (0, j)),
            pl.BlockSpec((D_MODEL, LANES), lambda i, j: (0, 0)),
        ],
        out_specs=out_specs,
        out_shape=out_shape,
        scratch_shapes=[pltpu.VMEM((tm, D_MODEL), BF16)],
        compiler_params=_cparams(("arbitrary", "arbitrary")),
        name="in_proj",
    )(x, norm_w, w_main, w_dt)


def _key_in_tile(v, n):
    return STREAM * lax.broadcasted_iota(jnp.int32, (SUBLANES, n), 0) + v


def _sublane_shift_up(x, k):
    sub = lax.broadcasted_iota(jnp.int32, x.shape, 0)
    return jnp.where(sub < SUBLANES - k, pltpu.roll(x, SUBLANES - k, 0), 0.0)


def _sublane_suffix(tot):
    s = tot
    for k in (1, 2, 4):
        s = s + _sublane_shift_up(s, k)
    return s


def _softplus(z):
    sign = jnp.uint32(0x80000000)
    neg_abs = lax.bitcast_convert_type(lax.bitcast_convert_type(z, jnp.uint32) | sign, F32)
    return jnp.maximum(z, 0.0) + jnp.log(1.0 + jnp.exp(neg_abs))


def _sb_tile(logits, bias, carry, valid):
    nv = KEY_TILE // SUBLANES
    keep = jnp.ones_like(carry)
    part = [None] * nv
    for v in range(nv - 1, -1, -1):
        beta = jax.nn.sigmoid(logits(v) + bias)
        if valid is not None:
            beta = jnp.where(valid(v), beta, 0.0)
        part[v] = beta * keep
        keep = keep * (1.0 - beta)
    drop = -jnp.log(keep)
    later = _sublane_suffix(_sublane_shift_up(drop, 1))
    scale = jnp.exp(-(later + carry))
    new_carry = carry + jnp.broadcast_to((later + drop)[0:1], carry.shape)
    return jnp.concatenate([part[v] * scale for v in range(nv)], axis=0), new_carry


def _prompt_attn_kernel(bias_ref, qt_ref, k_ref, vt_ref, o_ref, acc_ref, carry_ref, *sw_refs):
    p = pl.program_id(0)
    i = pl.program_id(1)
    n_groups = k_ref.shape[1] // TILES_PER_STEP
    n_col = Q_BLOCK // LANES
    s_refs = [sw_refs[2 * u:2 * u + 2] for u in range(TILES_PER_STEP)]
    w_refs = [sw_refs[2 * (TILES_PER_STEP + u):2 * (TILES_PER_STEP + u) + 2] for u in range(TILES_PER_STEP)]
    q2 = qt_ref[...]
    row = lax.broadcasted_iota(jnp.int32, q2.shape, 0)
    qpos = i * Q_BLOCK + lax.broadcasted_iota(jnp.int32, (SUBLANES, LANES), 1)
    qpads = [jnp.where((row // ATTN_HEAD_DIM) == hh, q2, jnp.zeros_like(q2)) for hh in range(2)]
    biases = [bias_ref[2 * p + hh] for hh in range(2)]
    hrows = [slice(ATTN_HEAD_DIM * hh, ATTN_HEAD_DIM * (hh + 1)) for hh in range(2)]

    def tile_of(g, u):
        return TILES_PER_STEP * g + (TILES_PER_STEP - 1 - u)

    def logits(g, u, hh):
        return jnp.dot(k_ref[0, tile_of(g, u)], qpads[hh], preferred_element_type=F32)

    def values(g, u, hh):
        return jnp.dot(vt_ref[0, tile_of(g, u), hrows[hh], :], w_refs[u][hh][...],
                       preferred_element_type=F32)

    def step(g_prev, g_cur, g_next, masked):
        units = [(u, hh) for u in range(TILES_PER_STEP) for hh in range(2)]
        pv = {uh: values(g_prev, *uh) for uh in units}
        s_next = {uh: logits(g_next, *uh) for uh in units}
        ws = {}
        for u, hh in units:
            halves = []
            for c in range(n_col):
                cols = slice(LANES * c, LANES * (c + 1))
                valid = None
                if masked:
                    def valid(v, u=u, c=c):
                        kpos = tile_of(g_cur, u) * KEY_TILE + _key_in_tile(v, LANES)
                        return (kpos < qpos + LANES * c) & (kpos >= PROMPT_PAD)
                w, nc = _sb_tile(
                    lambda v, u=u, hh=hh, cols=cols: s_refs[u][hh][SUBLANES * v:SUBLANES * (v + 1), cols],
                    biases[hh], carry_ref[hh, :, cols], valid)
                carry_ref[hh, :, cols] = nc
                halves.append(w)
            ws[u, hh] = jnp.concatenate(halves, axis=1).astype(BF16)
        for u, hh in units:
            w_refs[u][hh][...] = ws[u, hh]
            s_refs[u][hh][...] = s_next[u, hh]
            acc_ref[hrows[hh], :] += pv[u, hh]

    acc_ref[...] = jnp.zeros_like(acc_ref)
    carry_ref[...] = jnp.zeros_like(carry_ref)
    for u in range(TILES_PER_STEP):
        for hh in range(2):
            s_refs[u][hh][...] = logits(i, u, hh)
            w_refs[u][hh][...] = jnp.zeros_like(w_refs[u][hh])

    @pl.when(i > 0)
    def _():
        step(jnp.minimum(i + 1, n_groups - 1), i, i - 1, True)

        def full(n, c):
            g = i - 1 - n
            step(g + 1, g, g - 1, False)
            return c
        lax.fori_loop(0, i - 1, full, 0)

    step(1, 0, 0, True)
    for u in range(TILES_PER_STEP):
        for hh in range(2):
            acc_ref[hrows[hh], :] += values(0, u, hh)
    o_ref[...] = acc_ref[...].T


def _prompt_attention(bias, q_t, k_perm, v_t):
    t = q_t.shape[1]
    n_kt = t // KEY_TILE
    pair_w = 2 * ATTN_HEAD_DIM
    return pl.pallas_call(
        _prompt_attn_kernel,
        grid=(HEAD_PAIRS, t // Q_BLOCK),
        in_specs=[
            pl.BlockSpec(memory_space=pltpu.SMEM),
            pl.BlockSpec((pair_w, Q_BLOCK), lambda p, i: (p, i)),
            pl.BlockSpec((1, n_kt, KEY_TILE, pair_w), lambda p, i: (p, 0, 0, 0)),
            pl.BlockSpec((1, n_kt, pair_w, KEY_TILE), lambda p, i: (p, 0, 0, 0)),
        ],
        out_specs=pl.BlockSpec((Q_BLOCK, pair_w), lambda p, i: (i, p)),
        out_shape=jax.ShapeDtypeStruct((t, ATTN_WIDTH), F32),
        scratch_shapes=[pltpu.VMEM((pair_w, Q_BLOCK), F32), pltpu.VMEM((2, SUBLANES, Q_BLOCK), F32)]
        + [pltpu.VMEM((KEY_TILE, Q_BLOCK), F32)] * (2 * TILES_PER_STEP)
        + [pltpu.VMEM((KEY_TILE, Q_BLOCK), BF16)] * (2 * TILES_PER_STEP),
        compiler_params=_cparams(("arbitrary", "arbitrary")),
        name="prompt_attention",
    )(bias, q_t, k_perm, v_t)


def _sample_attn_kernel(pt_ref, qbd_ref, bias_ref, knew_ref, vnew_ref, *rest, dec_seq):
    n_pg = PAGES_PER_STEP
    k_refs = rest[:n_pg]
    v_refs = rest[n_pg:2 * n_pg]
    o_ref, acc_ref, carry_ref = rest[2 * n_pg:]
    s_id = pl.program_id(1)
    qbd = qbd_ref[0]
    bias = bias_ref[...]
    r_io = lax.broadcasted_iota(jnp.int32, (LANES, PAGE_SIZE), 0)
    c_io = lax.broadcasted_iota(jnp.int32, (LANES, PAGE_SIZE), 1)
    later_or_same = (r_io >= c_io).astype(BF16)

    def rows_of(ref):
        return ref[...].reshape(ATTN_WIDTH, PAGE_SIZE).astype(BF16)

    def tile(k_ref, v_ref, valid):
        z = jnp.dot(qbd, rows_of(k_ref), preferred_element_type=F32) + bias
        c = _softplus(z)
        if valid is not None:
            c = jnp.where(valid, c, 0.0)
        incl = sum(jnp.dot(part, later_or_same, preferred_element_type=F32) for part in _split3(c))
        carry = carry_ref[...]
        w = jnp.exp(z - (incl + carry))
        if valid is not None:
            w = jnp.where(valid, w, 0.0)
        carry_ref[...] = carry + jnp.broadcast_to(incl[:, 0:1], carry.shape)
        return rows_of(v_ref), w.T.astype(BF16)

    def accumulate(tiles):
        v_all = jnp.concatenate([v for v, _ in tiles], axis=1)
        w_all = jnp.concatenate([w for _, w in tiles], axis=0)
        acc_ref[...] += jnp.dot(v_all, w_all, preferred_element_type=F32)

    @pl.when(s_id == 0)
    def _():
        acc_ref[...] = jnp.zeros_like(acc_ref)
        carry_ref[...] = jnp.zeros_like(carry_ref)
        accumulate([tile(knew_ref, vnew_ref, c_io < r_io % dec_seq)])

    accumulate([tile(k_refs[r], v_refs[r], None) for r in range(n_pg)])

    @pl.when(s_id == pl.num_programs(1) - 1)
    def _():
        row = lax.broadcasted_iota(jnp.int32, acc_ref.shape, 0)
        lane = lax.broadcasted_iota(jnp.int32, acc_ref.shape, 1)
        own = jnp.where(row // ATTN_HEAD_DIM == lane // dec_seq, acc_ref[...], 0.0)
        fold = (r_io % dec_seq == c_io).astype(BF16)
        out_t = sum(jnp.dot(part, fold, preferred_element_type=F32) for part in _split3(own))
        o_ref[0] = out_t.T[0:dec_seq, :]


def _sample_attention(page_table, q_bd, bias_rows, k_new_t, v_new_t, cache_k_t, cache_v_t, dec_seq):
    b, n_pages = page_table.shape
    n_steps = n_pages // PAGES_PER_STEP
    page_block = (None, ATTN_HEADS, ATTN_HEAD_DIM, PAGE_SIZE)

    def page_spec(r):
        def idx(bi, si, pt):
            return (pt[bi, n_pages - 1 - (si * PAGES_PER_STEP + r)], 0, 0, 0)
        return pl.BlockSpec(page_block, idx)

    per_seq = lambda bi, si, pt: (bi, 0, 0)
    per_seq4 = lambda bi, si, pt: (bi, 0, 0, 0)
    grid_spec = pltpu.PrefetchScalarGridSpec(
        num_scalar_prefetch=1,
        grid=(b, n_steps),
        in_specs=[
            pl.BlockSpec((1, LANES, ATTN_WIDTH), per_seq),
            pl.BlockSpec((LANES, PAGE_SIZE), lambda bi, si, pt: (0, 0)),
            pl.BlockSpec(page_block, per_seq4),
            pl.BlockSpec(page_block, per_seq4),
        ] + [page_spec(r) for r in range(PAGES_PER_STEP)] * 2,
        out_specs=pl.BlockSpec((1, dec_seq, ATTN_WIDTH), per_seq),
        scratch_shapes=[pltpu.VMEM((ATTN_WIDTH, LANES), F32), pltpu.VMEM((LANES, PAGE_SIZE), F32)],
    )
    return pl.pallas_call(
        functools.partial(_sample_attn_kernel, dec_seq=dec_seq),
        grid_spec=grid_spec,
        out_shape=jax.ShapeDtypeStruct((b, dec_seq, ATTN_WIDTH), F32),
        compiler_params=_cparams(("arbitrary", "arbitrary")),
        name="sample_attention",
    )(page_table, q_bd, bias_rows, k_new_t, v_new_t,
      *([cache_k_t] * PAGES_PER_STEP), *([cache_v_t] * PAGES_PER_STEP))


def _split3(x):
    hi = x.astype(BF16)
    r1 = x - hi.astype(F32)
    mid = r1.astype(BF16)
    lo = (r1 - mid.astype(F32)).astype(BF16)
    return hi, mid, lo


def _ssd_kernel(xbc_ref, z_ref, dt_ref, h0_ref, hist_ref, convw_ref, convb_ref, dtb_ref, a_ref,
                dskip_ref, nw_ref, y_ref, hout_ref, buf_ref, xc_ref, dtbuf_ref, ybuf_ref, h_ref,
                *, rows, first_valid):
    c = pl.program_id(1)
    hist = SUBLANES

    @pl.when(c == 0)
    def _():
        h_ref[...] = h0_ref[0]
        buf_ref[...] = jnp.zeros_like(buf_ref)
        buf_ref[0:hist, :] = hist_ref[0]
        dtbuf_ref[...] = jnp.zeros_like(dtbuf_ref)

    buf_ref[hist:hist + rows, :] = xbc_ref[0]
    dtbuf_ref[0:rows, :] = dt_ref[0]

    conv = convb_ref[...]
    for tap in range(CONV_W):
        start = hist - (CONV_W - 1) + tap
        conv = conv + buf_ref[start:start + CHUNK, :] * convw_ref[tap:tap + 1, :]
    xc_ref[...] = conv * jax.nn.sigmoid(conv)
    buf_ref[0:hist, :] = buf_ref[rows:rows + hist, :]

    t_io = lax.broadcasted_iota(jnp.int32, (CHUNK, LANES), 0)
    s_io = lax.broadcasted_iota(jnp.int32, (CHUNK, LANES), 1)
    x_dt = dtbuf_ref[...] + dtb_ref[...]
    dt = jnp.maximum(x_dt, 0.0) + jnp.log1p(jnp.exp(-jnp.abs(x_dt)))
    live = (t_io < rows) & (c * CHUNK + t_io >= first_valid)
    dt = jnp.where(live, dt, 0.0)
    la = dt * a_ref[...]
    tril = (s_io <= t_io).astype(BF16)
    cum = sum(jnp.dot(tril, part, preferred_element_type=F32) for part in _split3(la))
    cum_t = cum.T
    causal = s_io <= t_io
    lane_lo = s_io < SSD_HEAD_DIM
    row_lo = t_io < SSD_HEAD_DIM

    def col(x, h):
        return jnp.broadcast_to(x[:, h:h + 1], (CHUNK, LANES))

    def pair_cols(x, h0):
        return jnp.where(lane_lo, col(x, h0), col(x, h0 + 1))

    heads_per_group = SSD_HEADS // SSD_GROUPS
    for g in range(SSD_GROUPS):
        b_g = xc_ref[:, SSD_INNER + D_STATE * g:SSD_INNER + D_STATE * (g + 1)].astype(BF16)
        c_g = xc_ref[:, SSD_INNER + GROUP_BC + D_STATE * g:
                     SSD_INNER + GROUP_BC + D_STATE * (g + 1)].astype(BF16)
        cb = lax.dot_general(c_g, b_g, (((1,), (1,)), ((), ())), preferred_element_type=F32)
        for pr in range(heads_per_group // 2):
            h0 = heads_per_group * g + 2 * pr
            lanes = slice(SSD_HEAD_DIM * h0, SSD_HEAD_DIM * (h0 + 2))
            xdt = xc_ref[:, lanes] * pair_cols(dt, h0)
            cum_p = pair_cols(cum, h0)
            y = None
            for hx, keep in ((h0, lane_lo), (h0 + 1, ~lane_lo)):
                seg = col(cum, hx) - jnp.broadcast_to(cum_t[hx:hx + 1, :], (CHUNK, LANES))
                decay = jnp.where(causal, jnp.exp(jnp.where(causal, seg, 0.0)), 0.0)
                m = (cb * decay).astype(BF16)
                xm = jnp.where(keep, xdt, 0.0).astype(BF16)
                d = jnp.dot(m, xm, preferred_element_type=F32)
                y = d if y is None else y + d
            h_pair = h_ref[lanes, :]
            y_off = lax.dot_general(c_g, h_pair.astype(BF16), (((1,), (1,)), ((), ())),
                                    preferred_element_type=F32)
            ybuf_ref[:, lanes] = y + y_off * jnp.exp(cum_p)
            cum_end = cum_p[CHUNK - 1:CHUNK, :]
            x_end = (xdt * jnp.exp(cum_end - cum_p)).T.astype(BF16)
            upd = jnp.dot(x_end, b_g, preferred_element_type=F32)
            end0 = jnp.broadcast_to(cum[CHUNK - 1:CHUNK, h0:h0 + 1], (CHUNK, LANES))
            end1 = jnp.broadcast_to(cum[CHUNK - 1:CHUNK, h0 + 1:h0 + 2], (CHUNK, LANES))
            h_ref[lanes, :] = jnp.exp(jnp.where(row_lo, end0, end1)) * h_pair + upd

    zz = z_ref[0]
    yv = (ybuf_ref[0:rows, :] + xc_ref[0:rows, 0:SSD_INNER] * dskip_ref[...]) * (zz * jax.nn.sigmoid(zz))
    gw = SSD_INNER // SSD_GROUPS
    outs = []
    for g in range(SSD_GROUPS):
        yg = yv[:, gw * g:gw * (g + 1)]
        outs.append(yg * _rms_scale(yg))
    y_ref[0] = (jnp.concatenate(outs, axis=1) * nw_ref[...]).astype(BF16)

    @pl.when(c == pl.num_programs(1) - 1)
    def _():
        hout_ref[0] = h_ref[...]


def _ssd(xbc, z, dt_raw, h0, hist, conv_w, conv_b, dt_bias, a_neg, d_skip, norm_w, *, rows, first_valid):
    b, l, _ = xbc.shape
    hp = SSD_HEADS * SSD_HEAD_DIM
    blk = lambda bi, ci: (bi, ci, 0)
    per_b = lambda bi, ci: (bi, 0, 0)
    const = lambda bi, ci: (0, 0)
    return pl.pallas_call(
        functools.partial(_ssd_kernel, rows=rows, first_valid=first_valid),
        grid=(b, l // rows),
        in_specs=[
            pl.BlockSpec((1, rows, CONV_DIM), blk),
            pl.BlockSpec((1, rows, SSD_INNER), blk),
            pl.BlockSpec((1, rows, LANES), blk),
            pl.BlockSpec((1, hp, D_STATE), per_b),
            pl.BlockSpec((1, SUBLANES, CONV_DIM), per_b),
            pl.BlockSpec((CONV_W, CONV_DIM), const),
            pl.BlockSpec((1, CONV_DIM), const),
            pl.BlockSpec((1, LANES), const),
            pl.BlockSpec((1, LANES), const),
            pl.BlockSpec((1, SSD_INNER), const),
            pl.BlockSpec((1, SSD_INNER), const),
        ],
        out_specs=(pl.BlockSpec((1, rows, SSD_INNER), blk), pl.BlockSpec((1, hp, D_STATE), per_b)),
        out_shape=(jax.ShapeDtypeStruct((b, l, SSD_INNER), BF16),
                   jax.ShapeDtypeStruct((b, hp, D_STATE), F32)),
        scratch_shapes=[
            pltpu.VMEM((SUBLANES + CHUNK, CONV_DIM), F32),
            pltpu.VMEM((CHUNK, CONV_DIM), F32),
            pltpu.VMEM((CHUNK, LANES), F32),
            pltpu.VMEM((CHUNK, SSD_INNER), F32),
            pltpu.VMEM((hp, D_STATE), F32),
        ],
        compiler_params=_cparams(("arbitrary", "arbitrary")),
        name="ssd",
    )(xbc, z, dt_raw, h0, hist, conv_w, conv_b, dt_bias, a_neg, d_skip, norm_w)


def _outproj_kernel(x_ref, attn_ref, ssd_ref, anw_ref, woa_ref, wos_ref, fnw_ref,
                    rwh_ref, rwl_ref, rb_ref, x2_ref, xn_ref, comb_ref):
    a = attn_ref[...]
    an = ((a * _rms_scale(a)) * anw_ref[...]).astype(BF16)
    d = jnp.dot(an, woa_ref[...], preferred_element_type=F32)
    d = d + jnp.dot(ssd_ref[...], wos_ref[...], preferred_element_type=F32)
    x2 = x_ref[...] + d
    x2_ref[...] = x2
    xn = (x2 * _rms_scale(x2)) * fnw_ref[...]
    xh = xn.astype(BF16)
    xl = (xn - xh.astype(F32)).astype(BF16)
    xn_ref[...] = xh

    logits = (jnp.dot(xh, rwh_ref[...], preferred_element_type=F32)
              + jnp.dot(xh, rwl_ref[...], preferred_element_type=F32)
              + jnp.dot(xl, rwh_ref[...], preferred_element_type=F32)) + rb_ref[...]
    lane = lax.broadcasted_iota(jnp.int32, logits.shape, 1)
    lane_f = lane.astype(F32)
    ninf = jnp.float32(-jnp.inf)
    far = jnp.float32(2 * LANES)

    def first_max(vals):
        m = jnp.max(vals, axis=1, keepdims=True)
        idx = jnp.min(jnp.where(vals == m, lane_f, far), axis=1, keepdims=True)
        return m, idx

    g_mask = (lane >= N_EXPERTS) & (lane < N_EXPERTS + N_EXPERT_GROUPS)
    gl = jnp.where(g_mask, logits, ninf)
    g_max, g_idx = first_max(gl)
    g_w = 1.0 / jnp.sum(jnp.where(g_mask, jnp.exp(gl - g_max), 0.0), axis=1, keepdims=True)
    g_sel = g_idx - N_EXPERTS
    e_mask = (lane < N_EXPERTS) & ((lane // EXPERTS_PER_GROUP).astype(F32) == g_sel)
    el = jnp.where(e_mask, logits, ninf)
    m1, i1 = first_max(el)
    m2, i2 = first_max(jnp.where(lane_f == i1, ninf, el))
    ex = jnp.exp(m2 - m1)
    w1 = g_w / (1.0 + ex)
    w2 = g_w * (ex / (1.0 + ex))
    comb_ref[...] = jnp.where(lane_f == i1, w1, jnp.where(lane_f == i2, w2, 0.0))


def _out_proj(x, attn, ssd, attn_norm_w, wo_a, wo_s, ffn_norm_w, rw_hi, rw_lo, rb, *, tm, row0):
    n = attn.shape[0] - row0 * tm
    rows_in = lambda i: (i + row0, 0)
    rows_out = lambda i: (i, 0)
    const = lambda i: (0, 0)
    return pl.pallas_call(
        _outproj_kernel,
        grid=(n // tm,),
        in_specs=[
            pl.BlockSpec((tm, D_MODEL), rows_in),
            pl.BlockSpec((tm, ATTN_WIDTH), rows_in),
            pl.BlockSpec((tm, SSD_INNER), rows_in),
            pl.BlockSpec((1, ATTN_WIDTH), const),
            pl.BlockSpec((ATTN_WIDTH, D_MODEL), const),
            pl.BlockSpec((SSD_INNER, D_MODEL), const),
            pl.BlockSpec((1, D_MODEL), const),
            pl.BlockSpec((D_MODEL, LANES), const),
            pl.BlockSpec((D_MODEL, LANES), const),
            pl.BlockSpec((1, LANES), const),
        ],
        out_specs=(pl.BlockSpec((tm, D_MODEL), rows_out), pl.BlockSpec((tm, D_MODEL), rows_out),
                   pl.BlockSpec((tm, LANES), rows_out)),
        out_shape=(jax.ShapeDtypeStruct((n, D_MODEL), F32), jax.ShapeDtypeStruct((n, D_MODEL), BF16),
                   jax.ShapeDtypeStruct((n, LANES), F32)),
        compiler_params=_cparams(("arbitrary",)),
        name="out_proj_router",
    )(x, attn, ssd, attn_norm_w, wo_a, wo_s, ffn_norm_w, rw_hi, rw_lo, rb)


def _moe_kernel(xn_ref, comb_ref, x2_ref, wg_ref, wu_ref, wd_ref, fw_ref, y_ref, acc_ref):
    e = pl.program_id(1)

    @pl.when(e == 0)
    def _():
        acc_ref[...] = jnp.zeros_like(acc_ref)

    x = xn_ref[...]
    g = jnp.dot(x, wg_ref[0], preferred_element_type=F32)
    u = jnp.dot(x, wu_ref[0], preferred_element_type=F32)
    comb = comb_ref[...]
    lane = lax.broadcasted_iota(jnp.int32, comb.shape, 1)
    c = jnp.sum(jnp.where(lane == e, comb, 0.0), axis=1, keepdims=True)
    h = ((g * jax.nn.sigmoid(g)) * u * c).astype(BF16)
    acc_ref[...] += jnp.dot(h, wd_ref[0], preferred_element_type=F32)

    @pl.when(e == pl.num_programs(1) - 1)
    def _():
        y = x2_ref[...] + acc_ref[...]
        y_ref[...] = (y * _rms_scale(y)) * fw_ref[...]


def _moe(xn, comb, x2, w_gate, w_up, w_down, final_w, *, tm):
    n = xn.shape[0]
    rows = lambda i, e: (i, 0)
    per_e = lambda i, e: (e, 0, 0)
    return pl.pallas_call(
        _moe_kernel,
        grid=(n // tm, N_EXPERTS),
        in_specs=[
            pl.BlockSpec((tm, D_MODEL), rows),
            pl.BlockSpec((tm, LANES), rows),
            pl.BlockSpec((tm, D_MODEL), rows),
            pl.BlockSpec((1, D_MODEL, D_EXPERT), per_e),
            pl.BlockSpec((1, D_MODEL, D_EXPERT), per_e),
            pl.BlockSpec((1, D_EXPERT, D_MODEL), per_e),
            pl.BlockSpec((1, D_MODEL), lambda i, e: (0, 0)),
        ],
        out_specs=pl.BlockSpec((tm, D_MODEL), rows),
        out_shape=jax.ShapeDtypeStruct((n, D_MODEL), F32),
        scratch_shapes=[pltpu.VMEM((tm, D_MODEL), F32)],
        compiler_params=_cparams(("arbitrary", "arbitrary")),
        name="moe",
    )(xn, comb, x2, w_gate, w_up, w_down, final_w)


def _largest_tile(n, cap):
    best = LANES
    for m in range(LANES, cap + 1, LANES):
        if n % m == 0:
            best = m
    return best


def _permute_key_tiles(x):
    t = x.shape[0]
    x = x.reshape(t // KEY_TILE, SUBLANES, STREAM, HEAD_PAIRS, 2 * ATTN_HEAD_DIM)
    return x.transpose(3, 0, 2, 1, 4).reshape(HEAD_PAIRS, t // KEY_TILE, KEY_TILE, 2 * ATTN_HEAD_DIM)


def kernel(x_prompt, x_sample, cache_k, cache_v, page_table, state_ssm, state_conv, meta_tokens,
           norm_mix_w, w_in, attn_logit_bias, conv_w, conv_b, dt_bias, a_log, d_skip, ssd_norm_w,
           attn_norm_w, w_out, norm_ffn_w, router_group_w, router_group_b, router_expert_w,
           router_expert_b, expert_w_gate, expert_w_up, expert_w_down, final_norm_w):
    assert w_in.shape[0] == 1 and x_prompt.shape[0] == 1
    seq = x_prompt.shape[1]
    dec_batch, dec_seq, _ = x_sample.shape
    assert seq % Q_BLOCK == 0 and dec_seq == SUBLANES

    w_in0 = w_in[0]
    n_main = 3 * ATTN_WIDTH + SSD_INNER + CONV_DIM
    w_main = w_in0[:, :n_main].astype(BF16)
    w_dt = jnp.pad(w_in0[:, n_main:], ((0, 0), (0, LANES - SSD_HEADS))).astype(BF16)
    nmw = norm_mix_w[0][None]
    pad_heads = lambda v: jnp.pad(v, (0, LANES - SSD_HEADS))[None]
    dtb = pad_heads(dt_bias[0])
    a_neg = pad_heads(-jnp.exp(a_log[0]))
    dskip = jnp.repeat(d_skip[0], SSD_HEAD_DIM)[None]
    ssd_nw = ssd_norm_w[0][None]
    conv_w0, conv_b0 = conv_w[0], conv_b[0][None]
    wo = w_out[0].astype(BF16)
    wo_a, wo_s = wo[:ATTN_WIDTH], wo[ATTN_WIDTH:]
    rw = jnp.concatenate([router_expert_w[0], router_group_w[0]], axis=1)
    rw = jnp.pad(rw, ((0, 0), (0, LANES - rw.shape[1])))
    rw_hi = rw.astype(BF16)
    rw_lo = (rw - rw_hi.astype(F32)).astype(BF16)
    rb = jnp.concatenate([router_expert_b[0], router_group_b[0]])
    rb = jnp.pad(rb, (0, LANES - rb.shape[0]))[None]
    wg, wu, wd = (w[0].astype(BF16) for w in (expert_w_gate, expert_w_up, expert_w_down))
    anw, fnw, finw = attn_norm_w[0][None], norm_ffn_w[0][None], final_norm_w[None]
    bias = attn_logit_bias[0]

    def tail(x, attn, ssd, tm, row0):
        x2, xn, comb = _out_proj(x, attn, ssd, anw, wo_a, wo_s, fnw, rw_hi, rw_lo, rb, tm=tm, row0=row0)
        return _moe(xn, comb, x2, wg, wu, wd, finw, tm=_largest_tile(xn.shape[0], 512))

    xp = jnp.concatenate([jnp.zeros((PROMPT_PAD, D_MODEL), F32), meta_tokens.astype(F32), x_prompt[0]], axis=0)
    tp = xp.shape[0]
    q_p, k32_p, v32_p, kb_p, vb_p, z_p, xbc_p, dt_p = _in_proj(xp, nmw, w_main, w_dt, _largest_tile(tp, 640))
    k_perm = _permute_key_tiles(kb_p.reshape(tp, HEAD_PAIRS, 2 * ATTN_HEAD_DIM))
    v_t = _permute_key_tiles(vb_p.reshape(tp, HEAD_PAIRS, 2 * ATTN_HEAD_DIM)).transpose(0, 1, 3, 2)
    attn_p = _prompt_attention(bias, q_p.T, k_perm, v_t)
    hp = SSD_HEADS * SSD_HEAD_DIM
    ssd_p, h_p = _ssd(xbc_p[None], z_p[None], dt_p[None],
                      jnp.zeros((1, hp, D_STATE), F32), jnp.zeros((1, SUBLANES, CONV_DIM), F32),
                      conv_w0, conv_b0, dtb, a_neg, dskip, ssd_nw, rows=CHUNK, first_valid=PROMPT_PAD)
    y_prompt = tail(xp, attn_p, ssd_p[0], Q_BLOCK, 1)

    n_s = dec_batch * dec_seq
    xs = x_sample.reshape(n_s, D_MODEL)
    q_s, k32_s, v32_s, _, _, z_s, xbc_s, dt_s = _in_proj(xs, nmw, w_main, w_dt, _largest_tile(n_s, 512))
    q4 = q_s.reshape(dec_batch, dec_seq, ATTN_HEADS, ATTN_HEAD_DIM)
    eye = jnp.eye(ATTN_HEADS, dtype=BF16)
    q_bd = q4.transpose(0, 2, 1, 3)[:, :, :, None, :] * eye[None, :, None, :, None]
    q_bd = q_bd.reshape(dec_batch, ATTN_HEADS * dec_seq, ATTN_WIDTH)
    bias_rows = jnp.broadcast_to(jnp.repeat(bias, dec_seq)[:, None], (ATTN_HEADS * dec_seq, PAGE_SIZE))

    def pages_t(x):
        return jnp.moveaxis(x, -3, -1)

    def spread(x):
        x = pages_t(x.reshape(dec_batch, dec_seq, ATTN_HEADS, ATTN_HEAD_DIM))
        return jnp.pad(x, ((0, 0), (0, 0), (0, 0), (0, PAGE_SIZE - dec_seq)))

    attn_s = _sample_attention(page_table, q_bd, bias_rows, spread(k32_s), spread(v32_s),
                               pages_t(cache_k[0]), pages_t(cache_v[0]), dec_seq)
    hist_s = jnp.pad(state_conv[0], ((0, 0), (SUBLANES - (CONV_W - 1), 0), (0, 0)))
    ssd_s, h_s = _ssd(xbc_s.reshape(dec_batch, dec_seq, CONV_DIM), z_s.reshape(dec_batch, dec_seq, SSD_INNER),
                      dt_s.reshape(dec_batch, dec_seq, LANES),
                      state_ssm[0].reshape(dec_batch, hp, D_STATE), hist_s,
                      conv_w0, conv_b0, dtb, a_neg, dskip, ssd_nw, rows=dec_seq, first_valid=0)
    y_sample = tail(xs, attn_s.reshape(n_s, ATTN_WIDTH), ssd_s.reshape(n_s, SSD_INNER), _largest_tile(n_s, 256), 0)

    heads = (ATTN_HEADS, ATTN_HEAD_DIM)
    state = (SSD_HEADS, SSD_HEAD_DIM, D_STATE)
    xbc_s3 = xbc_s.reshape(dec_batch, dec_seq, CONV_DIM)
    return (
        y_prompt[None],
        y_sample.reshape(dec_batch, dec_seq, D_MODEL),
        k32_p[PROMPT_PAD:].reshape(1, 1, tp - PROMPT_PAD, *heads),
        v32_p[PROMPT_PAD:].reshape(1, 1, tp - PROMPT_PAD, *heads),
        h_p.reshape(1, 1, *state).astype(state_ssm.dtype),
        xbc_p[tp - (CONV_W - 1):][None, None],
        k32_s.reshape(1, dec_batch, dec_seq, *heads),
        v32_s.reshape(1, dec_batch, dec_seq, *heads),
        h_s.reshape(1, dec_batch, *state).astype(state_ssm.dtype),
        xbc_s3[:, dec_seq - (CONV_W - 1):][None],
    )
```

```python
import functools

import jax
import jax.numpy as jnp
from jax import lax
from jax.experimental import pallas as pl
from jax.experimental.pallas import tpu as pltpu

F32 = jnp.float32
BF16 = jnp.bfloat16

LANES = 128
SUBLANES = 8
VMEM_LIMIT = 56 * 1024 * 1024

N_META = 16
D_MODEL = 1024
ATTN_HEADS = 16
ATTN_HEAD_DIM = 64
ATTN_WIDTH = ATTN_HEADS * ATTN_HEAD_DIM
ATTN_SCALE = ATTN_HEAD_DIM ** -0.5
HEAD_PAIRS = ATTN_HEADS // 2
SSD_INNER = 2 * D_MODEL
SSD_HEAD_DIM = 64
SSD_HEADS = SSD_INNER // SSD_HEAD_DIM
SSD_GROUPS = 4
D_STATE = 128
CONV_W = 4
GROUP_BC = SSD_GROUPS * D_STATE
CONV_DIM = SSD_INNER + 2 * GROUP_BC
N_EXPERT_GROUPS = 4
EXPERTS_PER_GROUP = 8
N_EXPERTS = N_EXPERT_GROUPS * EXPERTS_PER_GROUP
D_EXPERT = 512
PAGE_SIZE = 128
EPS = 1e-6

Q_BLOCK = 256
KEY_TILE = 128
STREAM = KEY_TILE // SUBLANES
TILES_PER_STEP = Q_BLOCK // KEY_TILE
CHUNK = 128
PROMPT_PAD = Q_BLOCK - N_META
PAGES_PER_STEP = 8


def _rms_scale(x):
    return lax.rsqrt(jnp.mean(x * x, axis=-1, keepdims=True) + EPS)


def _cparams(sem):
    return pltpu.CompilerParams(dimension_semantics=sem, vmem_limit_bytes=VMEM_LIMIT)


def _inproj_kernel(x_ref, nw_ref, w_ref, wdt_ref,
                   q_ref, k32_ref, v32_ref, kb_ref, vb_ref, z_ref, xbc_ref, dt_ref, xn_ref):
    j = pl.program_id(1)

    @pl.when(j == 0)
    def _():
        x = x_ref[...]
        xn_ref[...] = ((x * _rms_scale(x)) * nw_ref[...]).astype(BF16)

    res = jnp.dot(xn_ref[...], w_ref[...], preferred_element_type=F32)

    @pl.when(j == 0)
    def _():
        q_ref[...] = (res * ATTN_SCALE).astype(BF16)

    @pl.when(j == 1)
    def _():
        k32_ref[...] = res
        kb_ref[...] = res.astype(BF16)

    @pl.when(j == 2)
    def _():
        v32_ref[...] = res
        vb_ref[...] = res.astype(BF16)

    @pl.when((j == 3) | (j == 4))
    def _():
        z_ref[...] = res

    @pl.when(j >= 5)
    def _():
        xbc_ref[...] = res

    @pl.when(j == 7)
    def _():
        dt_ref[...] = jnp.dot(xn_ref[...], wdt_ref[...], preferred_element_type=F32)


def _in_proj(x, norm_w, w_main, w_dt, tm):
    t = x.shape[0]
    nb = D_MODEL
    n_col = w_main.shape[1] // nb
    row = lambda i, j: (i, 0)
    out_shape = (
        jax.ShapeDtypeStruct((t, ATTN_WIDTH), BF16),
        jax.ShapeDtypeStruct((t, ATTN_WIDTH), F32),
        jax.ShapeDtypeStruct((t, ATTN_WIDTH), F32),
        jax.ShapeDtypeStruct((t, ATTN_WIDTH), BF16),
        jax.ShapeDtypeStruct((t, ATTN_WIDTH), BF16),
        jax.ShapeDtypeStruct((t, SSD_INNER), F32),
        jax.ShapeDtypeStruct((t, CONV_DIM), F32),
        jax.ShapeDtypeStruct((t, LANES), F32),
    )
    out_specs = (
        pl.BlockSpec((tm, nb), row), pl.BlockSpec((tm, nb), row), pl.BlockSpec((tm, nb), row),
        pl.BlockSpec((tm, nb), row), pl.BlockSpec((tm, nb), row),
        pl.BlockSpec((tm, nb), lambda i, j: (i, jnp.clip(j - 3, 0, 1))),
        pl.BlockSpec((tm, nb), lambda i, j: (i, jnp.clip(j - 5, 0, 2))),
        pl.BlockSpec((tm, LANES), row),
    )
    return pl.pallas_call(
        _inproj_kernel,
        grid=(t // tm, n_col),
        in_specs=[
            pl.BlockSpec((tm, D_MODEL), row),
            pl.BlockSpec((1, D_MODEL), lambda i, j: (0, 0)),
            pl.BlockSpec((D_MODEL, nb), lambda i, j: (0, j)),
            pl.BlockSpec((D_MODEL, LANES), lambda i, j: (0, 0)),
        ],
        out_specs=out_specs,
        out_shape=out_shape,
        scratch_shapes=[pltpu.VMEM((tm, D_MODEL), BF16)],
        compiler_params=_cparams(("arbitrary", "arbitrary")),
        name="in_proj",
    )(x, norm_w, w_main, w_dt)


def _key_in_tile(v, n):
    return STREAM * lax.broadcasted_iota(jnp.int32, (SUBLANES, n), 0) + v


def _sublane_shift_up(x, k):
    sub = lax.broadcasted_iota(jnp.int32, x.shape, 0)
    return jnp.where(sub < SUBLANES - k, pltpu.roll(x, SUBLANES - k, 0), 0.0)


def _sublane_suffix(tot):
    s = tot
    for k in (1, 2, 4):
        s = s + _sublane_shift_up(s, k)
    return s


def _softplus(z):
    sign = jnp.uint32(0x80000000)
    neg_abs = lax.bitcast_convert_type(lax.bitcast_convert_type(z, jnp.uint32) | sign, F32)
    return jnp.maximum(z, 0.0) + jnp.log(1.0 + jnp.exp(neg_abs))


def _sb_tile(logits, bias, carry, valid):
    nv = KEY_TILE // SUBLANES
    keep = jnp.ones_like(carry)
    part = [None] * nv
    for v in range(nv - 1, -1, -1):
        beta = jax.nn.sigmoid(logits(v) + bias)
        if valid is not None:
            beta = jnp.where(valid(v), beta, 0.0)
        part[v] = beta * keep
        keep = keep * (1.0 - beta)
        if v % 2 == 0:
            part[v] = jnp.concatenate([part[v], part[v + 1]], axis=0).astype(BF16)
            part[v + 1] = None
    drop = -jnp.log(keep)
    later = _sublane_suffix(_sublane_shift_up(drop, 1))
    scale = jnp.exp(-(later + carry))
    new_carry = carry + jnp.broadcast_to((later + drop)[0:1], carry.shape)
    scale2 = jnp.concatenate([scale, scale], axis=0).astype(BF16)
    return jnp.concatenate([part[v] * scale2 for v in range(0, nv, 2)], axis=0), new_carry


def _prompt_attn_kernel(bias_ref, qt_ref, k_ref, vt_ref, o_ref, acc_ref, carry_ref, *sw_refs):
    p = pl.program_id(0)
    i = pl.program_id(1)
    n_groups = k_ref.shape[1] // TILES_PER_STEP
    n_col = Q_BLOCK // LANES
    w_refs = sw_refs[0:2]
    s_refs = [[sw_refs[2 + n_col * (2 * u + hh):2 + n_col * (2 * u + hh + 1)] for hh in range(2)]
              for u in range(TILES_PER_STEP)]
    q2 = qt_ref[...]
    row = lax.broadcasted_iota(jnp.int32, q2.shape, 0)
    qpos = i * Q_BLOCK + lax.broadcasted_iota(jnp.int32, (SUBLANES, LANES), 1)
    qpads = [jnp.where((row // ATTN_HEAD_DIM) == hh, q2, jnp.zeros_like(q2)) for hh in range(2)]
    biases = [bias_ref[2 * p + hh] for hh in range(2)]
    hrows = [slice(ATTN_HEAD_DIM * hh, ATTN_HEAD_DIM * (hh + 1)) for hh in range(2)]

    def tile_of(g, u):
        return TILES_PER_STEP * g + (TILES_PER_STEP - 1 - u)

    def logits(g, u, hh):
        return jnp.dot(k_ref[0, tile_of(g, u)], qpads[hh], preferred_element_type=F32)

    def store_logits(u, hh, s):
        for c in range(n_col):
            s_refs[u][hh][c][...] = s[:, LANES * c:LANES * (c + 1)]

    def values(g, hh):
        v_t = jnp.concatenate([vt_ref[0, tile_of(g, u), hrows[hh], :] for u in range(TILES_PER_STEP)], axis=1)
        return jnp.dot(v_t, w_refs[hh][...], preferred_element_type=F32)

    def step(g_prev, g_cur, g_next, masked):
        pv = [values(g_prev, hh) for hh in range(2)]
        s_next = {(u, hh): logits(g_next, u, hh) for u in range(TILES_PER_STEP) for hh in range(2)}
        ws = {}
        for u in range(TILES_PER_STEP):
            for hh in range(2):
                halves = []
                for c in range(n_col):
                    valid = None
                    if masked:
                        def valid(v, u=u, c=c):
                            kpos = tile_of(g_cur, u) * KEY_TILE + _key_in_tile(v, LANES)
                            return (kpos < qpos + LANES * c) & (kpos >= PROMPT_PAD)
                    def slab(v, u=u, hh=hh, c=c):
                        return s_refs[u][hh][c][SUBLANES * v:SUBLANES * (v + 1), :]
                    w, nc = _sb_tile(slab, biases[hh], carry_ref[hh, c], valid)
                    carry_ref[hh, c] = nc
                    halves.append(w)
                ws[u, hh] = jnp.concatenate(halves, axis=1).astype(BF16)
        for hh in range(2):
            for u in range(TILES_PER_STEP):
                w_refs[hh][KEY_TILE * u:KEY_TILE * (u + 1), :] = ws[u, hh]
                store_logits(u, hh, s_next[u, hh])
            acc_ref[hrows[hh], :] += pv[hh]

    acc_ref[...] = jnp.zeros_like(acc_ref)
    carry_ref[...] = jnp.zeros_like(carry_ref)
    for hh in range(2):
        for u in range(TILES_PER_STEP):
            store_logits(u, hh, logits(i, u, hh))
        w_refs[hh][...] = jnp.zeros_like(w_refs[hh])

    @pl.when(i > 0)
    def _():
        step(jnp.minimum(i + 1, n_groups - 1), i, i - 1, True)

        def full(n, c):
            g = i - 1 - 2 * n
            step(g + 1, g, g - 1, False)
            step(g, g - 1, g - 2, False)
            return c
        lax.fori_loop(0, (i - 1) // 2, full, 0)

        @pl.when((i - 1) % 2 == 1)
        def _():
            step(2, 1, 0, False)

    step(1, 0, 0, True)
    for hh in range(2):
        acc_ref[hrows[hh], :] += values(0, hh)
    o_ref[...] = acc_ref[...].T


def _prompt_attention(bias, q_t, k_perm, v_t):
    t = q_t.shape[1]
    n_kt = t // KEY_TILE
    pair_w = 2 * ATTN_HEAD_DIM
    return pl.pallas_call(
        _prompt_attn_kernel,
        grid=(HEAD_PAIRS, t // Q_BLOCK),
        in_specs=[
            pl.BlockSpec(memory_space=pltpu.SMEM),
            pl.BlockSpec((pair_w, Q_BLOCK), lambda p, i: (p, i)),
            pl.BlockSpec((1, n_kt, KEY_TILE, pair_w), lambda p, i: (p, 0, 0, 0)),
            pl.BlockSpec((1, n_kt, pair_w, KEY_TILE), lambda p, i: (p, 0, 0, 0)),
        ],
        out_specs=pl.BlockSpec((Q_BLOCK, pair_w), lambda p, i: (i, p)),
        out_shape=jax.ShapeDtypeStruct((t, ATTN_WIDTH), F32),
        scratch_shapes=[pltpu.VMEM((pair_w, Q_BLOCK), F32),
                        pltpu.VMEM((2, Q_BLOCK // LANES, SUBLANES, LANES), F32)]
        + [pltpu.VMEM((TILES_PER_STEP * KEY_TILE, Q_BLOCK), BF16)] * 2
        + [pltpu.VMEM((KEY_TILE, LANES), F32)] * (2 * TILES_PER_STEP * (Q_BLOCK // LANES)),
        compiler_params=_cparams(("arbitrary", "arbitrary")),
        name="prompt_attention",
    )(bias, q_t, k_perm, v_t)


def _sample_attn_kernel(pt_ref, qbd_ref, bias_ref, knew_ref, vnew_ref, *rest, dec_seq):
    n_pg = PAGES_PER_STEP
    k_refs = rest[:n_pg]
    v_refs = rest[n_pg:2 * n_pg]
    o_ref, acc_ref, carry_ref = rest[2 * n_pg:]
    s_id = pl.program_id(1)
    qbd = qbd_ref[0]
    bias = bias_ref[...]
    r_io = lax.broadcasted_iota(jnp.int32, (LANES, PAGE_SIZE), 0)
    c_io = lax.broadcasted_iota(jnp.int32, (LANES, PAGE_SIZE), 1)
    later_or_same = (r_io >= c_io).astype(BF16)

    def rows_of(ref):
        return ref[...].reshape(ATTN_WIDTH, PAGE_SIZE).astype(BF16)

    def tile(k_ref, v_ref, valid):
        z = jnp.dot(qbd, rows_of(k_ref), preferred_element_type=F32) + bias
        c = _softplus(z)
        if valid is not None:
            c = jnp.where(valid, c, 0.0)
        incl = sum(jnp.dot(part, later_or_same, preferred_element_type=F32) for part in _split3(c))
        carry = carry_ref[...]
        w = jnp.exp(z - (incl + carry))
        if valid is not None:
            w = jnp.where(valid, w, 0.0)
        carry_ref[...] = carry + jnp.broadcast_to(incl[:, 0:1], carry.shape)
        return rows_of(v_ref), w.T.astype(BF16)

    def accumulate(tiles):
        v_all = jnp.concatenate([v for v, _ in tiles], axis=1)
        w_all = jnp.concatenate([w for _, w in tiles], axis=0)
        acc_ref[...] += jnp.dot(v_all, w_all, preferred_element_type=F32)

    @pl.when(s_id == 0)
    def _():
        acc_ref[...] = jnp.zeros_like(acc_ref)
        carry_ref[...] = jnp.zeros_like(carry_ref)
        accumulate([tile(knew_ref, vnew_ref, c_io < r_io % dec_seq)])

    accumulate([tile(k_refs[r], v_refs[r], None) for r in range(n_pg)])

    @pl.when(s_id == pl.num_programs(1) - 1)
    def _():
        row = lax.broadcasted_iota(jnp.int32, acc_ref.shape, 0)
        lane = lax.broadcasted_iota(jnp.int32, acc_ref.shape, 1)
        own = jnp.where(row // ATTN_HEAD_DIM == lane // dec_seq, acc_ref[...], 0.0)
        fold = (r_io % dec_seq == c_io).astype(BF16)
        out_t = sum(jnp.dot(part, fold, preferred_element_type=F32) for part in _split3(own))
        o_ref[0] = out_t.T[0:dec_seq, :]


def _sample_attention(page_table, q_bd, bias_rows, k_new_t, v_new_t, cache_k_t, cache_v_t, dec_seq):
    b, n_pages = page_table.shape
    n_steps = n_pages // PAGES_PER_STEP
    page_block = (None, ATTN_HEADS, ATTN_HEAD_DIM, PAGE_SIZE)

    def page_spec(r):
        def idx(bi, si, pt):
            return (pt[bi, n_pages - 1 - (si * PAGES_PER_STEP + r)], 0, 0, 0)
        return pl.BlockSpec(page_block, idx)

    per_seq = lambda bi, si, pt: (bi, 0, 0)
    per_seq4 = lambda bi, si, pt: (bi, 0, 0, 0)
    grid_spec = pltpu.PrefetchScalarGridSpec(
        num_scalar_prefetch=1,
        grid=(b, n_steps),
        in_specs=[
            pl.BlockSpec((1, LANES, ATTN_WIDTH), per_seq),
            pl.BlockSpec((LANES, PAGE_SIZE), lambda bi, si, pt: (0, 0)),
            pl.BlockSpec(page_block, per_seq4),
            pl.BlockSpec(page_block, per_seq4),
        ] + [page_spec(r) for r in range(PAGES_PER_STEP)] * 2,
        out_specs=pl.BlockSpec((1, dec_seq, ATTN_WIDTH), per_seq),
        scratch_shapes=[pltpu.VMEM((ATTN_WIDTH, LANES), F32), pltpu.VMEM((LANES, PAGE_SIZE), F32)],
    )
    return pl.pallas_call(
        functools.partial(_sample_attn_kernel, dec_seq=dec_seq),
        grid_spec=grid_spec,
        out_shape=jax.ShapeDtypeStruct((b, dec_seq, ATTN_WIDTH), F32),
        compiler_params=_cparams(("arbitrary", "arbitrary")),
        name="sample_attention",
    )(page_table, q_bd, bias_rows, k_new_t, v_new_t,
      *([cache_k_t] * PAGES_PER_STEP), *([cache_v_t] * PAGES_PER_STEP))


def _split3(x):
    hi = x.astype(BF16)
    r1 = x - hi.astype(F32)
    mid = r1.astype(BF16)
    lo = (r1 - mid.astype(F32)).astype(BF16)
    return hi, mid, lo


def _ssd_kernel(xbc_ref, z_ref, dt_ref, h0_ref, hist_ref, convw_ref, convb_ref, dtb_ref, a_ref,
                dskip_ref, nw_ref, y_ref, hout_ref, buf_ref, xc_ref, dtbuf_ref, ybuf_ref, h_ref,
                *, rows, first_valid):
    c = pl.program_id(1)
    hist = SUBLANES

    @pl.when(c == 0)
    def _():
        h_ref[...] = h0_ref[0]
        buf_ref[...] = jnp.zeros_like(buf_ref)
        buf_ref[0:hist, :] = hist_ref[0]
        dtbuf_ref[...] = jnp.zeros_like(dtbuf_ref)

    buf_ref[hist:hist + rows, :] = xbc_ref[0]
    dtbuf_ref[0:rows, :] = dt_ref[0]

    conv = convb_ref[...]
    for tap in range(CONV_W):
        start = hist - (CONV_W - 1) + tap
        conv = conv + buf_ref[start:start + CHUNK, :] * convw_ref[tap:tap + 1, :]
    xc_ref[...] = conv * jax.nn.sigmoid(conv)
    buf_ref[0:hist, :] = buf_ref[rows:rows + hist, :]

    t_io = lax.broadcasted_iota(jnp.int32, (CHUNK, LANES), 0)
    s_io = lax.broadcasted_iota(jnp.int32, (CHUNK, LANES), 1)
    x_dt = dtbuf_ref[...] + dtb_ref[...]
    dt = jnp.maximum(x_dt, 0.0) + jnp.log1p(jnp.exp(-jnp.abs(x_dt)))
    live = (t_io < rows) & (c * CHUNK + t_io >= first_valid)
    dt = jnp.where(live, dt, 0.0)
    la = dt * a_ref[...]
    tril = (s_io <= t_io).astype(BF16)
    cum = sum(jnp.dot(tril, part, preferred_element_type=F32) for part in _split3(la))
    cum_t = cum.T
    causal = s_io <= t_io
    lane_lo = s_io < SSD_HEAD_DIM
    row_lo = t_io < SSD_HEAD_DIM

    def col(x, h):
        return jnp.broadcast_to(x[:, h:h + 1], (CHUNK, LANES))

    def pair_cols(x, h0):
        return jnp.where(lane_lo, col(x, h0), col(x, h0 + 1))

    heads_per_group = SSD_HEADS // SSD_GROUPS
    for g in range(SSD_GROUPS):
        b_g = xc_ref[:, SSD_INNER + D_STATE * g:SSD_INNER + D_STATE * (g + 1)].astype(BF16)
        c_g = xc_ref[:, SSD_INNER + GROUP_BC + D_STATE * g:
                     SSD_INNER + GROUP_BC + D_STATE * (g + 1)].astype(BF16)
        cb = lax.dot_general(c_g, b_g, (((1,), (1,)), ((), ())), preferred_element_type=F32)
        for pr in range(heads_per_group // 2):
            h0 = heads_per_group * g + 2 * pr
            lanes = slice(SSD_HEAD_DIM * h0, SSD_HEAD_DIM * (h0 + 2))
            xdt = xc_ref[:, lanes] * pair_cols(dt, h0)
            cum_p = pair_cols(cum, h0)
            y = None
            for hx, keep in ((h0, lane_lo), (h0 + 1, ~lane_lo)):
                seg = col(cum, hx) - jnp.broadcast_to(cum_t[hx:hx + 1, :], (CHUNK, LANES))
                decay = jnp.where(causal, jnp.exp(jnp.where(causal, seg, 0.0)), 0.0)
                m = (cb * decay).astype(BF16)
                xm = jnp.where(keep, xdt, 0.0).astype(BF16)
                d = jnp.dot(m, xm, preferred_element_type=F32)
                y = d if y is None else y + d
            h_pair = h_ref[lanes, :]
            y_off = lax.dot_general(c_g, h_pair.astype(BF16), (((1,), (1,)), ((), ())),
                                    preferred_element_type=F32)
            ybuf_ref[:, lanes] = y + y_off * jnp.exp(cum_p)
            cum_end = cum_p[CHUNK - 1:CHUNK, :]
            x_end = (xdt * jnp.exp(cum_end - cum_p)).T.astype(BF16)
            upd = jnp.dot(x_end, b_g, preferred_element_type=F32)
            end0 = jnp.broadcast_to(cum[CHUNK - 1:CHUNK, h0:h0 + 1], (CHUNK, LANES))
            end1 = jnp.broadcast_to(cum[CHUNK - 1:CHUNK, h0 + 1:h0 + 2], (CHUNK, LANES))
            h_ref[lanes, :] = jnp.exp(jnp.where(row_lo, end0, end1)) * h_pair + upd

    zz = z_ref[0]
    yv = (ybuf_ref[0:rows, :] + xc_ref[0:rows, 0:SSD_INNER] * dskip_ref[...]) * (zz * jax.nn.sigmoid(zz))
    gw = SSD_INNER // SSD_GROUPS
    outs = []
    for g in range(SSD_GROUPS):
        yg = yv[:, gw * g:gw * (g + 1)]
        outs.append(yg * _rms_scale(yg))
    y_ref[0] = (jnp.concatenate(outs, axis=1) * nw_ref[...]).astype(BF16)

    @pl.when(c == pl.num_programs(1) - 1)
    def _():
        hout_ref[0] = h_ref[...]


def _ssd(xbc, z, dt_raw, h0, hist, conv_w, conv_b, dt_bias, a_neg, d_skip, norm_w, *, rows, first_valid):
    b, l, _ = xbc.shape
    hp = SSD_HEADS * SSD_HEAD_DIM
    blk = lambda bi, ci: (bi, ci, 0)
    per_b = lambda bi, ci: (bi, 0, 0)
    const = lambda bi, ci: (0, 0)
    return pl.pallas_call(
        functools.partial(_ssd_kernel, rows=rows, first_valid=first_valid),
        grid=(b, l // rows),
        in_specs=[
            pl.BlockSpec((1, rows, CONV_DIM), blk),
            pl.BlockSpec((1, rows, SSD_INNER), blk),
            pl.BlockSpec((1, rows, LANES), blk),
            pl.BlockSpec((1, hp, D_STATE), per_b),
            pl.BlockSpec((1, SUBLANES, CONV_DIM), per_b),
            pl.BlockSpec((CONV_W, CONV_DIM), const),
            pl.BlockSpec((1, CONV_DIM), const),
            pl.BlockSpec((1, LANES), const),
            pl.BlockSpec((1, LANES), const),
            pl.BlockSpec((1, SSD_INNER), const),
            pl.BlockSpec((1, SSD_INNER), const),
        ],
        out_specs=(pl.BlockSpec((1, rows, SSD_INNER), blk), pl.BlockSpec((1, hp, D_STATE), per_b)),
        out_shape=(jax.ShapeDtypeStruct((b, l, SSD_INNER), BF16),
                   jax.ShapeDtypeStruct((b, hp, D_STATE), F32)),
        scratch_shapes=[
            pltpu.VMEM((SUBLANES + CHUNK, CONV_DIM), F32),
            pltpu.VMEM((CHUNK, CONV_DIM), F32),
            pltpu.VMEM((CHUNK, LANES), F32),
            pltpu.VMEM((CHUNK, SSD_INNER), F32),
            pltpu.VMEM((hp, D_STATE), F32),
        ],
        compiler_params=_cparams(("arbitrary", "arbitrary")),
        name="ssd",
    )(xbc, z, dt_raw, h0, hist, conv_w, conv_b, dt_bias, a_neg, d_skip, norm_w)


def _outproj_kernel(x_ref, attn_ref, ssd_ref, anw_ref, woa_ref, wos_ref, fnw_ref,
                    rwh_ref, rwl_ref, rb_ref, x2_ref, xn_ref, comb_ref):
    a = attn_ref[...]
    an = ((a * _rms_scale(a)) * anw_ref[...]).astype(BF16)
    d = jnp.dot(an, woa_ref[...], preferred_element_type=F32)
    d = d + jnp.dot(ssd_ref[...], wos_ref[...], preferred_element_type=F32)
    x2 = x_ref[...] + d
    x2_ref[...] = x2
    xn = (x2 * _rms_scale(x2)) * fnw_ref[...]
    xh = xn.astype(BF16)
    xl = (xn - xh.astype(F32)).astype(BF16)
    xn_ref[...] = xh

    logits = (jnp.dot(xh, rwh_ref[...], preferred_element_type=F32)
              + jnp.dot(xh, rwl_ref[...], preferred_element_type=F32)
              + jnp.dot(xl, rwh_ref[...], preferred_element_type=F32)) + rb_ref[...]
    lane = lax.broadcasted_iota(jnp.int32, logits.shape, 1)
    lane_f = lane.astype(F32)
    ninf = jnp.float32(-jnp.inf)
    far = jnp.float32(2 * LANES)

    def first_max(vals):
        m = jnp.max(vals, axis=1, keepdims=True)
        idx = jnp.min(jnp.where(vals == m, lane_f, far), axis=1, keepdims=True)
        return m, idx

    g_mask = (lane >= N_EXPERTS) & (lane < N_EXPERTS + N_EXPERT_GROUPS)
    gl = jnp.where(g_mask, logits, ninf)
    g_max, g_idx = first_max(gl)
    g_w = 1.0 / jnp.sum(jnp.where(g_mask, jnp.exp(gl - g_max), 0.0), axis=1, keepdims=True)
    g_sel = g_idx - N_EXPERTS
    e_mask = (lane < N_EXPERTS) & ((lane // EXPERTS_PER_GROUP).astype(F32) == g_sel)
    el = jnp.where(e_mask, logits, ninf)
    m1, i1 = first_max(el)
    m2, i2 = first_max(jnp.where(lane_f == i1, ninf, el))
    ex = jnp.exp(m2 - m1)
    w1 = g_w / (1.0 + ex)
    w2 = g_w * (ex / (1.0 + ex))
    comb_ref[...] = jnp.where(lane_f == i1, w1, jnp.where(lane_f == i2, w2, 0.0))


def _out_proj(x, attn, ssd, attn_norm_w, wo_a, wo_s, ffn_norm_w, rw_hi, rw_lo, rb, *, tm, row0):
    n = attn.shape[0] - row0 * tm
    rows_in = lambda i: (i + row0, 0)
    rows_out = lambda i: (i, 0)
    const = lambda i: (0, 0)
    return pl.pallas_call(
        _outproj_kernel,
        grid=(n // tm,),
        in_specs=[
            pl.BlockSpec((tm, D_MODEL), rows_in),
            pl.BlockSpec((tm, ATTN_WIDTH), rows_in),
            pl.BlockSpec((tm, SSD_INNER), rows_in),
            pl.BlockSpec((1, ATTN_WIDTH), const),
            pl.BlockSpec((ATTN_WIDTH, D_MODEL), const),
            pl.BlockSpec((SSD_INNER, D_MODEL), const),
            pl.BlockSpec((1, D_MODEL), const),
            pl.BlockSpec((D_MODEL, LANES), const),
            pl.BlockSpec((D_MODEL, LANES), const),
            pl.BlockSpec((1, LANES), const),
        ],
        out_specs=(pl.BlockSpec((tm, D_MODEL), rows_out), pl.BlockSpec((tm, D_MODEL), rows_out),
                   pl.BlockSpec((tm, LANES), rows_out)),
        out_shape=(jax.ShapeDtypeStruct((n, D_MODEL), F32), jax.ShapeDtypeStruct((n, D_MODEL), BF16),
                   jax.ShapeDtypeStruct((n, LANES), F32)),
        compiler_params=_cparams(("arbitrary",)),
        name="out_proj_router",
    )(x, attn, ssd, attn_norm_w, wo_a, wo_s, ffn_norm_w, rw_hi, rw_lo, rb)


def _moe_kernel(xn_ref, comb_ref, x2_ref, wg_ref, wu_ref, wd_ref, fw_ref, y_ref, acc_ref):
    e = pl.program_id(1)

    @pl.when(e == 0)
    def _():
        acc_ref[...] = jnp.zeros_like(acc_ref)

    x = xn_ref[...]
    g = jnp.dot(x, wg_ref[0], preferred_element_type=F32)
    u = jnp.dot(x, wu_ref[0], preferred_element_type=F32)
    comb = comb_ref[...]
    lane = lax.broadcasted_iota(jnp.int32, comb.shape, 1)
    c = jnp.sum(jnp.where(lane == e, comb, 0.0), axis=1, keepdims=True)
    h = ((g * jax.nn.sigmoid(g)) * u * c).astype(BF16)
    acc_ref[...] += jnp.dot(h, wd_ref[0], preferred_element_type=F32)

    @pl.when(e == pl.num_programs(1) - 1)
    def _():
        y = x2_ref[...] + acc_ref[...]
        y_ref[...] = (y * _rms_scale(y)) * fw_ref[...]


def _moe(xn, comb, x2, w_gate, w_up, w_down, final_w, *, tm):
    n = xn.shape[0]
    rows = lambda i, e: (i, 0)
    per_e = lambda i, e: (e, 0, 0)
    return pl.pallas_call(
        _moe_kernel,
        grid=(n // tm, N_EXPERTS),
        in_specs=[
            pl.BlockSpec((tm, D_MODEL), rows),
            pl.BlockSpec((tm, LANES), rows),
            pl.BlockSpec((tm, D_MODEL), rows),
            pl.BlockSpec((1, D_MODEL, D_EXPERT), per_e),
            pl.BlockSpec((1, D_MODEL, D_EXPERT), per_e),
            pl.BlockSpec((1, D_EXPERT, D_MODEL), per_e),
            pl.BlockSpec((1, D_MODEL), lambda i, e: (0, 0)),
        ],
        out_specs=pl.BlockSpec((tm, D_MODEL), rows),
        out_shape=jax.ShapeDtypeStruct((n, D_MODEL), F32),
        scratch_shapes=[pltpu.VMEM((tm, D_MODEL), F32)],
        compiler_params=_cparams(("arbitrary", "arbitrary")),
        name="moe",
    )(xn, comb, x2, w_gate, w_up, w_down, final_w)


def _largest_tile(n, cap):
    best = LANES
    for m in range(LANES, cap + 1, LANES):
        if n % m == 0:
            best = m
    return best


def _permute_key_tiles(x):
    t = x.shape[0]
    x = x.reshape(t // KEY_TILE, SUBLANES, STREAM, HEAD_PAIRS, 2 * ATTN_HEAD_DIM)
    return x.transpose(3, 0, 2, 1, 4).reshape(HEAD_PAIRS, t // KEY_TILE, KEY_TILE, 2 * ATTN_HEAD_DIM)


def kernel(x_prompt, x_sample, cache_k, cache_v, page_table, state_ssm, state_conv, meta_tokens,
           norm_mix_w, w_in, attn_logit_bias, conv_w, conv_b, dt_bias, a_log, d_skip, ssd_norm_w,
           attn_norm_w, w_out, norm_ffn_w, router_group_w, router_group_b, router_expert_w,
           router_expert_b, expert_w_gate, expert_w_up, expert_w_down, final_norm_w):
    assert w_in.shape[0] == 1 and x_prompt.shape[0] == 1
    seq = x_prompt.shape[1]
    dec_batch, dec_seq, _ = x_sample.shape
    assert seq % Q_BLOCK == 0 and dec_seq == SUBLANES

    w_in0 = w_in[0]
    n_main = 3 * ATTN_WIDTH + SSD_INNER + CONV_DIM
    w_main = w_in0[:, :n_main].astype(BF16)
    w_dt = jnp.pad(w_in0[:, n_main:], ((0, 0), (0, LANES - SSD_HEADS))).astype(BF16)
    nmw = norm_mix_w[0][None]
    pad_heads = lambda v: jnp.pad(v, (0, LANES - SSD_HEADS))[None]
    dtb = pad_heads(dt_bias[0])
    a_neg = pad_heads(-jnp.exp(a_log[0]))
    dskip = jnp.repeat(d_skip[0], SSD_HEAD_DIM)[None]
    ssd_nw = ssd_norm_w[0][None]
    conv_w0, conv_b0 = conv_w[0], conv_b[0][None]
    wo = w_out[0].astype(BF16)
    wo_a, wo_s = wo[:ATTN_WIDTH], wo[ATTN_WIDTH:]
    rw = jnp.concatenate([router_expert_w[0], router_group_w[0]], axis=1)
    rw = jnp.pad(rw, ((0, 0), (0, LANES - rw.shape[1])))
    rw_hi = rw.astype(BF16)
    rw_lo = (rw - rw_hi.astype(F32)).astype(BF16)
    rb = jnp.concatenate([router_expert_b[0], router_group_b[0]])
    rb = jnp.pad(rb, (0, LANES - rb.shape[0]))[None]
    wg, wu, wd = (w[0].astype(BF16) for w in (expert_w_gate, expert_w_up, expert_w_down))
    anw, fnw, finw = attn_norm_w[0][None], norm_ffn_w[0][None], final_norm_w[None]
    bias = attn_logit_bias[0]

    def tail(x, attn, ssd, tm, row0):
        x2, xn, comb = _out_proj(x, attn, ssd, anw, wo_a, wo_s, fnw, rw_hi, rw_lo, rb, tm=tm, row0=row0)
        return _moe(xn, comb, x2, wg, wu, wd, finw, tm=_largest_tile(xn.shape[0], 512))

    xp = jnp.concatenate([jnp.zeros((PROMPT_PAD, D_MODEL), F32), meta_tokens.astype(F32), x_prompt[0]], axis=0)
    tp = xp.shape[0]
    q_p, k32_p, v32_p, kb_p, vb_p, z_p, xbc_p, dt_p = _in_proj(xp, nmw, w_main, w_dt, _largest_tile(tp, 640))
    k_perm = _permute_key_tiles(kb_p.reshape(tp, HEAD_PAIRS, 2 * ATTN_HEAD_DIM))
    v_t = _permute_key_tiles(vb_p.reshape(tp, HEAD_PAIRS, 2 * ATTN_HEAD_DIM)).transpose(0, 1, 3, 2)
    attn_p = _prompt_attention(bias, q_p.T, k_perm, v_t)
    hp = SSD_HEADS * SSD_HEAD_DIM
    ssd_p, h_p = _ssd(xbc_p[None], z_p[None], dt_p[None],
                      jnp.zeros((1, hp, D_STATE), F32), jnp.zeros((1, SUBLANES, CONV_DIM), F32),
                      conv_w0, conv_b0, dtb, a_neg, dskip, ssd_nw, rows=CHUNK, first_valid=PROMPT_PAD)
    y_prompt = tail(xp, attn_p, ssd_p[0], Q_BLOCK, 1)

    n_s = dec_batch * dec_seq
    xs = x_sample.reshape(n_s, D_MODEL)
    q_s, k32_s, v32_s, _, _, z_s, xbc_s, dt_s = _in_proj(xs, nmw, w_main, w_dt, _largest_tile(n_s, 512))
    q4 = q_s.reshape(dec_batch, dec_seq, ATTN_HEADS, ATTN_HEAD_DIM)
    eye = jnp.eye(ATTN_HEADS, dtype=BF16)
    q_bd = q4.transpose(0, 2, 1, 3)[:, :, :, None, :] * eye[None, :, None, :, None]
    q_bd = q_bd.reshape(dec_batch, ATTN_HEADS * dec_seq, ATTN_WIDTH)
    bias_rows = jnp.broadcast_to(jnp.repeat(bias, dec_seq)[:, None], (ATTN_HEADS * dec_seq, PAGE_SIZE))

    def pages_t(x):
        return jnp.moveaxis(x, -3, -1)

    def spread(x):
        x = pages_t(x.reshape(dec_batch, dec_seq, ATTN_HEADS, ATTN_HEAD_DIM))
        return jnp.pad(x, ((0, 0), (0, 0), (0, 0), (0, PAGE_SIZE - dec_seq)))

    attn_s = _sample_attention(page_table, q_bd, bias_rows, spread(k32_s), spread(v32_s),
                               pages_t(cache_k[0]), pages_t(cache_v[0]), dec_seq)
    hist_s = jnp.pad(state_conv[0], ((0, 0), (SUBLANES - (CONV_W - 1), 0), (0, 0)))
    ssd_s, h_s = _ssd(xbc_s.reshape(dec_batch, dec_seq, CONV_DIM), z_s.reshape(dec_batch, dec_seq, SSD_INNER),
                      dt_s.reshape(dec_batch, dec_seq, LANES),
                      state_ssm[0].reshape(dec_batch, hp, D_STATE), hist_s,
                      conv_w0, conv_b0, dtb, a_neg, dskip, ssd_nw, rows=dec_seq, first_valid=0)
    y_sample = tail(xs, attn_s.reshape(n_s, ATTN_WIDTH), ssd_s.reshape(n_s, SSD_INNER), _largest_tile(n_s, 256), 0)

    heads = (ATTN_HEADS, ATTN_HEAD_DIM)
    state = (SSD_HEADS, SSD_HEAD_DIM, D_STATE)
    xbc_s3 = xbc_s.reshape(dec_batch, dec_seq, CONV_DIM)
    return (
        y_prompt[None],
        y_sample.reshape(dec_batch, dec_seq, D_MODEL),
        k32_p[PROMPT_PAD:].reshape(1, 1, tp - PROMPT_PAD, *heads),
        v32_p[PROMPT_PAD:].reshape(1, 1, tp - PROMPT_PAD, *heads),
        h_p.reshape(1, 1, *state).astype(state_ssm.dtype),
        xbc_p[tp - (CONV_W - 1):][None, None],
        k32_s.reshape(1, dec_batch, dec_seq, *heads),
        v32_s.reshape(1, dec_batch, dec_seq, *heads),
        h_s.reshape(1, dec_batch, *state).astype(state_ssm.dtype),
        xbc_s3[:, dec_seq - (CONV_W - 1):][None],
    )
```

```python
import functools

import jax
import jax.numpy as jnp
from jax import lax
from jax.experimental import pallas as pl
from jax.experimental.pallas import tpu as pltpu

F32 = jnp.float32
BF16 = jnp.bfloat16

LANES = 128
SUBLANES = 8
VMEM_LIMIT = 56 * 1024 * 1024

N_META = 16
D_MODEL = 1024
ATTN_HEADS = 16
ATTN_HEAD_DIM = 64
ATTN_WIDTH = ATTN_HEADS * ATTN_HEAD_DIM
ATTN_SCALE = ATTN_HEAD_DIM ** -0.5
HEAD_PAIRS = ATTN_HEADS // 2
SSD_INNER = 2 * D_MODEL
SSD_HEAD_DIM = 64
SSD_HEADS = SSD_INNER // SSD_HEAD_DIM
SSD_GROUPS = 4
D_STATE = 128
CONV_W = 4
GROUP_BC = SSD_GROUPS * D_STATE
CONV_DIM = SSD_INNER + 2 * GROUP_BC
N_EXPERT_GROUPS = 4
EXPERTS_PER_GROUP = 8
N_EXPERTS = N_EXPERT_GROUPS * EXPERTS_PER_GROUP
D_EXPERT = 512
PAGE_SIZE = 128
EPS = 1e-6

Q_BLOCK = 256
KEY_TILE = 128
STREAM = KEY_TILE // SUBLANES
TILES_PER_STEP = Q_BLOCK // KEY_TILE
CHUNK = 128
PROMPT_PAD = Q_BLOCK - N_META
PAGES_PER_STEP = 8


def _rms_scale(x):
    return lax.rsqrt(jnp.mean(x * x, axis=-1, keepdims=True) + EPS)


def _cparams(sem):
    return pltpu.CompilerParams(dimension_semantics=sem, vmem_limit_bytes=VMEM_LIMIT)


def _inproj_kernel(x_ref, nw_ref, w_ref, wdt_ref,
                   q_ref, k32_ref, v32_ref, kb_ref, vb_ref, z_ref, xbc_ref, dt_ref, xn_ref):
    j = pl.program_id(1)

    @pl.when(j == 0)
    def _():
        x = x_ref[...]
        xn_ref[...] = ((x * _rms_scale(x)) * nw_ref[...]).astype(BF16)

    res = jnp.dot(xn_ref[...], w_ref[...], preferred_element_type=F32)

    @pl.when(j == 0)
    def _():
        q_ref[...] = (res * ATTN_SCALE).astype(BF16)

    @pl.when(j == 1)
    def _():
        k32_ref[...] = res
        kb_ref[...] = res.astype(BF16)

    @pl.when(j == 2)
    def _():
        v32_ref[...] = res
        vb_ref[...] = res.astype(BF16)

    @pl.when((j == 3) | (j == 4))
    def _():
        z_ref[...] = res

    @pl.when(j >= 5)
    def _():
        xbc_ref[...] = res

    @pl.when(j == 7)
    def _():
        dt_ref[...] = jnp.dot(xn_ref[...], wdt_ref[...], preferred_element_type=F32)


def _in_proj(x, norm_w, w_main, w_dt, tm):
    t = x.shape[0]
    nb = D_MODEL
    n_col = w_main.shape[1] // nb
    row = lambda i, j: (i, 0)
    out_shape = (
        jax.ShapeDtypeStruct((t, ATTN_WIDTH), BF16),
        jax.ShapeDtypeStruct((t, ATTN_WIDTH), F32),
        jax.ShapeDtypeStruct((t, ATTN_WIDTH), F32),
        jax.ShapeDtypeStruct((t, ATTN_WIDTH), BF16),
        jax.ShapeDtypeStruct((t, ATTN_WIDTH), BF16),
        jax.ShapeDtypeStruct((t, SSD_INNER), F32),
        jax.ShapeDtypeStruct((t, CONV_DIM), F32),
        jax.ShapeDtypeStruct((t, LANES), F32),
    )
    out_specs = (
        pl.BlockSpec((tm, nb), row), pl.BlockSpec((tm, nb), row), pl.BlockSpec((tm, nb), row),
        pl.BlockSpec((tm, nb), row), pl.BlockSpec((tm, nb), row),
        pl.BlockSpec((tm, nb), lambda i, j: (i, jnp.clip(j - 3, 0, 1))),
        pl.BlockSpec((tm, nb), lambda i, j: (i, jnp.clip(j - 5, 0, 2))),
        pl.BlockSpec((tm, LANES), row),
    )
    return pl.pallas_call(
        _inproj_kernel,
        grid=(t // tm, n_col),
        in_specs=[
            pl.BlockSpec((tm, D_MODEL), row),
            pl.BlockSpec((1, D_MODEL), lambda i, j: (0, 0)),
            pl.BlockSpec((D_MODEL, nb), lambda i, j: (0, j)),
            pl.BlockSpec((D_MODEL, LANES), lambda i, j: (0, 0)),
        ],
        out_specs=out_specs,
        out_shape=out_shape,
        scratch_shapes=[pltpu.VMEM((tm, D_MODEL), BF16)],
        compiler_params=_cparams(("arbitrary", "arbitrary")),
        name="in_proj",
    )(x, norm_w, w_main, w_dt)


def _key_in_tile(v, n):
    return STREAM * lax.broadcasted_iota(jnp.int32, (SUBLANES, n), 0) + v


def _sublane_shift_up(x, k):
    sub = lax.broadcasted_iota(jnp.int32, x.shape, 0)
    return jnp.where(sub < SUBLANES - k, pltpu.roll(x, SUBLANES - k, 0), 0.0)


def _sublane_suffix(tot):
    s = tot
    for k in (1, 2, 4):
        s = s + _sublane_shift_up(s, k)
    return s


def _softplus(z):
    sign = jnp.uint32(0x80000000)
    neg_abs = lax.bitcast_convert_type(lax.bitcast_convert_type(z, jnp.uint32) | sign, F32)
    return jnp.maximum(z, 0.0) + jnp.log(1.0 + jnp.exp(neg_abs))


def _sb_tile(logits, bias, carry, valid):
    nv = KEY_TILE // SUBLANES
    keep = jnp.ones_like(carry)
    part = [None] * nv
    for v in range(nv - 1, -1, -1):
        beta = jax.nn.sigmoid(logits(v) + bias)
        if valid is not None:
            beta = jnp.where(valid(v), beta, 0.0)
        part[v] = beta * keep
        keep = keep - part[v]
        if v % 2 == 0:
            part[v] = jnp.concatenate([part[v], part[v + 1]], axis=0).astype(BF16)
            part[v + 1] = None
    drop = -jnp.log(keep)
    later = _sublane_suffix(_sublane_shift_up(drop, 1))
    scale = jnp.exp(-(later + carry))
    new_carry = carry + jnp.broadcast_to((later + drop)[0:1], carry.shape)
    scale2 = jnp.concatenate([scale, scale], axis=0).astype(BF16)
    return jnp.concatenate([part[v] * scale2 for v in range(0, nv, 2)], axis=0), new_carry


def _prompt_attn_kernel(bias_ref, qt_ref, k_ref, vt_ref, o_ref, acc_ref, carry_ref, *sw_refs):
    p = pl.program_id(0)
    i = pl.program_id(1)
    n_groups = k_ref.shape[1] // TILES_PER_STEP
    n_col = Q_BLOCK // LANES
    w_refs = sw_refs[0:2]
    s_refs = [[sw_refs[2 + n_col * (2 * u + hh):2 + n_col * (2 * u + hh + 1)] for hh in range(2)]
              for u in range(TILES_PER_STEP)]
    q2 = qt_ref[...]
    row = lax.broadcasted_iota(jnp.int32, q2.shape, 0)
    qpos = i * Q_BLOCK + lax.broadcasted_iota(jnp.int32, (SUBLANES, LANES), 1)
    qpads = [jnp.where((row // ATTN_HEAD_DIM) == hh, q2, jnp.zeros_like(q2)) for hh in range(2)]
    biases = [bias_ref[2 * p + hh] for hh in range(2)]
    hrows = [slice(ATTN_HEAD_DIM * hh, ATTN_HEAD_DIM * (hh + 1)) for hh in range(2)]

    def tile_of(g, u):
        return TILES_PER_STEP * g + (TILES_PER_STEP - 1 - u)

    def logits(g, u, hh):
        return jnp.dot(k_ref[0, tile_of(g, u)], qpads[hh], preferred_element_type=F32)

    def store_logits(u, hh, s):
        for c in range(n_col):
            s_refs[u][hh][c][...] = s[:, LANES * c:LANES * (c + 1)]

    def values(g, hh):
        v_t = jnp.concatenate([vt_ref[0, tile_of(g, u), hrows[hh], :] for u in range(TILES_PER_STEP)], axis=1)
        return jnp.dot(v_t, w_refs[hh][...], preferred_element_type=F32)

    def step(g_prev, g_cur, g_next, masked):
        pv = [values(g_prev, hh) for hh in range(2)]
        s_next = {(u, hh): logits(g_next, u, hh) for u in range(TILES_PER_STEP) for hh in range(2)}
        ws = {}
        for u in range(TILES_PER_STEP):
            for hh in range(2):
                halves = []
                for c in range(n_col):
                    valid = None
                    if masked:
                        def valid(v, u=u, c=c):
                            kpos = tile_of(g_cur, u) * KEY_TILE + _key_in_tile(v, LANES)
                            return (kpos < qpos + LANES * c) & (kpos >= PROMPT_PAD)
                    def slab(v, u=u, hh=hh, c=c):
                        return s_refs[u][hh][c][SUBLANES * v:SUBLANES * (v + 1), :]
                    w, nc = _sb_tile(slab, biases[hh], carry_ref[hh, c], valid)
                    carry_ref[hh, c] = nc
                    halves.append(w)
                ws[u, hh] = jnp.concatenate(halves, axis=1).astype(BF16)
        for hh in range(2):
            for u in range(TILES_PER_STEP):
                w_refs[hh][KEY_TILE * u:KEY_TILE * (u + 1), :] = ws[u, hh]
                store_logits(u, hh, s_next[u, hh])
            acc_ref[hrows[hh], :] += pv[hh]

    acc_ref[...] = jnp.zeros_like(acc_ref)
    carry_ref[...] = jnp.zeros_like(carry_ref)
    for hh in range(2):
        for u in range(TILES_PER_STEP):
            store_logits(u, hh, logits(i, u, hh))
        w_refs[hh][...] = jnp.zeros_like(w_refs[hh])

    @pl.when(i > 0)
    def _():
        step(jnp.minimum(i + 1, n_groups - 1), i, i - 1, True)

        def full(n, c):
            g = i - 1 - 2 * n
            step(g + 1, g, g - 1, False)
            step(g, g - 1, g - 2, False)
            return c
        lax.fori_loop(0, (i - 1) // 2, full, 0)

        @pl.when((i - 1) % 2 == 1)
        def _():
            step(2, 1, 0, False)

    step(1, 0, 0, True)
    for hh in range(2):
        acc_ref[hrows[hh], :] += values(0, hh)
    o_ref[...] = acc_ref[...].T


def _prompt_attention(bias, q_t, k_perm, v_t):
    t = q_t.shape[1]
    n_kt = t // KEY_TILE
    pair_w = 2 * ATTN_HEAD_DIM
    return pl.pallas_call(
        _prompt_attn_kernel,
        grid=(HEAD_PAIRS, t // Q_BLOCK),
        in_specs=[
            pl.BlockSpec(memory_space=pltpu.SMEM),
            pl.BlockSpec((pair_w, Q_BLOCK), lambda p, i: (p, i)),
            pl.BlockSpec((1, n_kt, KEY_TILE, pair_w), lambda p, i: (p, 0, 0, 0)),
            pl.BlockSpec((1, n_kt, pair_w, KEY_TILE), lambda p, i: (p, 0, 0, 0)),
        ],
        out_specs=pl.BlockSpec((Q_BLOCK, pair_w), lambda p, i: (i, p)),
        out_shape=jax.ShapeDtypeStruct((t, ATTN_WIDTH), F32),
        scratch_shapes=[pltpu.VMEM((pair_w, Q_BLOCK), F32),
                        pltpu.VMEM((2, Q_BLOCK // LANES, SUBLANES, LANES), F32)]
        + [pltpu.VMEM((TILES_PER_STEP * KEY_TILE, Q_BLOCK), BF16)] * 2
        + [pltpu.VMEM((KEY_TILE, LANES), F32)] * (2 * TILES_PER_STEP * (Q_BLOCK // LANES)),
        compiler_params=_cparams(("arbitrary", "arbitrary")),
        name="prompt_attention",
    )(bias, q_t, k_perm, v_t)


def _sample_attn_kernel(pt_ref, qbd_ref, bias_ref, knew_ref, vnew_ref, *rest, dec_seq):
    n_pg = PAGES_PER_STEP
    k_refs = rest[:n_pg]
    v_refs = rest[n_pg:2 * n_pg]
    o_ref, acc_ref, carry_ref = rest[2 * n_pg:]
    s_id = pl.program_id(1)
    qbd = qbd_ref[0]
    bias = bias_ref[...]
    r_io = lax.broadcasted_iota(jnp.int32, (LANES, PAGE_SIZE), 0)
    c_io = lax.broadcasted_iota(jnp.int32, (LANES, PAGE_SIZE), 1)
    later_or_same = (r_io >= c_io).astype(BF16)

    def rows_of(ref):
        return ref[...].reshape(ATTN_WIDTH, PAGE_SIZE).astype(BF16)

    def tile(k_ref, v_ref, valid):
        z = jnp.dot(qbd, rows_of(k_ref), preferred_element_type=F32) + bias
        c = _softplus(z)
        if valid is not None:
            c = jnp.where(valid, c, 0.0)
        incl = sum(jnp.dot(part, later_or_same, preferred_element_type=F32) for part in _split3(c))
        carry = carry_ref[...]
        w = jnp.exp(z - (incl + carry))
        if valid is not None:
            w = jnp.where(valid, w, 0.0)
        carry_ref[...] = carry + jnp.broadcast_to(incl[:, 0:1], carry.shape)
        return rows_of(v_ref), w.T.astype(BF16)

    def accumulate(tiles):
        v_all = jnp.concatenate([v for v, _ in tiles], axis=1)
        w_all = jnp.concatenate([w for _, w in tiles], axis=0)
        acc_ref[...] += jnp.dot(v_all, w_all, preferred_element_type=F32)

    @pl.when(s_id == 0)
    def _():
        acc_ref[...] = jnp.zeros_like(acc_ref)
        carry_ref[...] = jnp.zeros_like(carry_ref)
        accumulate([tile(knew_ref, vnew_ref, c_io < r_io % dec_seq)])

    accumulate([tile(k_refs[r], v_refs[r], None) for r in range(n_pg)])

    @pl.when(s_id == pl.num_programs(1) - 1)
    def _():
        row = lax.broadcasted_iota(jnp.int32, acc_ref.shape, 0)
        lane = lax.broadcasted_iota(jnp.int32, acc_ref.shape, 1)
        own = jnp.where(row // ATTN_HEAD_DIM == lane // dec_seq, acc_ref[...], 0.0)
        fold = (r_io % dec_seq == c_io).astype(BF16)
        out_t = sum(jnp.dot(part, fold, preferred_element_type=F32) for part in _split3(own))
        o_ref[0] = out_t.T[0:dec_seq, :]


def _sample_attention(page_table, q_bd, bias_rows, k_new_t, v_new_t, cache_k_t, cache_v_t, dec_seq):
    b, n_pages = page_table.shape
    n_steps = n_pages // PAGES_PER_STEP
    page_block = (None, ATTN_HEADS, ATTN_HEAD_DIM, PAGE_SIZE)

    def page_spec(r):
        def idx(bi, si, pt):
            return (pt[bi, n_pages - 1 - (si * PAGES_PER_STEP + r)], 0, 0, 0)
        return pl.BlockSpec(page_block, idx)

    per_seq = lambda bi, si, pt: (bi, 0, 0)
    per_seq4 = lambda bi, si, pt: (bi, 0, 0, 0)
    grid_spec = pltpu.PrefetchScalarGridSpec(
        num_scalar_prefetch=1,
        grid=(b, n_steps),
        in_specs=[
            pl.BlockSpec((1, LANES, ATTN_WIDTH), per_seq),
            pl.BlockSpec((LANES, PAGE_SIZE), lambda bi, si, pt: (0, 0)),
            pl.BlockSpec(page_block, per_seq4),
            pl.BlockSpec(page_block, per_seq4),
        ] + [page_spec(r) for r in range(PAGES_PER_STEP)] * 2,
        out_specs=pl.BlockSpec((1, dec_seq, ATTN_WIDTH), per_seq),
        scratch_shapes=[pltpu.VMEM((ATTN_WIDTH, LANES), F32), pltpu.VMEM((LANES, PAGE_SIZE), F32)],
    )
    return pl.pallas_call(
        functools.partial(_sample_attn_kernel, dec_seq=dec_seq),
        grid_spec=grid_spec,
        out_shape=jax.ShapeDtypeStruct((b, dec_seq, ATTN_WIDTH), F32),
        compiler_params=_cparams(("arbitrary", "arbitrary")),
        name="sample_attention",
    )(page_table, q_bd, bias_rows, k_new_t, v_new_t,
      *([cache_k_t] * PAGES_PER_STEP), *([cache_v_t] * PAGES_PER_STEP))


def _split3(x):
    hi = x.astype(BF16)
    r1 = x - hi.astype(F32)
    mid = r1.astype(BF16)
    lo = (r1 - mid.astype(F32)).astype(BF16)
    return hi, mid, lo


def _ssd_kernel(xbc_ref, z_ref, dt_ref, h0_ref, hist_ref, convw_ref, convb_ref, dtb_ref, a_ref,
                dskip_ref, nw_ref, y_ref, hout_ref, buf_ref, xc_ref, dtbuf_ref, ybuf_ref, h_ref,
                *, rows, first_valid):
    c = pl.program_id(1)
    hist = SUBLANES

    @pl.when(c == 0)
    def _():
        h_ref[...] = h0_ref[0]
        buf_ref[...] = jnp.zeros_like(buf_ref)
        buf_ref[0:hist, :] = hist_ref[0]
        dtbuf_ref[...] = jnp.zeros_like(dtbuf_ref)

    buf_ref[hist:hist + rows, :] = xbc_ref[0]
    dtbuf_ref[0:rows, :] = dt_ref[0]

    conv = convb_ref[...]
    for tap in range(CONV_W):
        start = hist - (CONV_W - 1) + tap
        conv = conv + buf_ref[start:start + CHUNK, :] * convw_ref[tap:tap + 1, :]
    xc_ref[...] = conv * jax.nn.sigmoid(conv)
    buf_ref[0:hist, :] = buf_ref[rows:rows + hist, :]

    t_io = lax.broadcasted_iota(jnp.int32, (CHUNK, LANES), 0)
    s_io = lax.broadcasted_iota(jnp.int32, (CHUNK, LANES), 1)
    x_dt = dtbuf_ref[...] + dtb_ref[...]
    dt = jnp.maximum(x_dt, 0.0) + jnp.log1p(jnp.exp(-jnp.abs(x_dt)))
    live = (t_io < rows) & (c * CHUNK + t_io >= first_valid)
    dt = jnp.where(live, dt, 0.0)
    la = dt * a_ref[...]
    tril = (s_io <= t_io).astype(BF16)
    cum = sum(jnp.dot(tril, part, preferred_element_type=F32) for part in _split3(la))
    cum_t = cum.T
    causal = s_io <= t_io
    lane_lo = s_io < SSD_HEAD_DIM
    row_lo = t_io < SSD_HEAD_DIM

    def col(x, h):
        return jnp.broadcast_to(x[:, h:h + 1], (CHUNK, LANES))

    def pair_cols(x, h0):
        return jnp.where(lane_lo, col(x, h0), col(x, h0 + 1))

    heads_per_group = SSD_HEADS // SSD_GROUPS
    for g in range(SSD_GROUPS):
        b_g = xc_ref[:, SSD_INNER + D_STATE * g:SSD_INNER + D_STATE * (g + 1)].astype(BF16)
        c_g = xc_ref[:, SSD_INNER + GROUP_BC + D_STATE * g:
                     SSD_INNER + GROUP_BC + D_STATE * (g + 1)].astype(BF16)
        cb = lax.dot_general(c_g, b_g, (((1,), (1,)), ((), ())), preferred_element_type=F32)
        for pr in range(heads_per_group // 2):
            h0 = heads_per_group * g + 2 * pr
            lanes = slice(SSD_HEAD_DIM * h0, SSD_HEAD_DIM * (h0 + 2))
            xdt = xc_ref[:, lanes] * pair_cols(dt, h0)
            cum_p = pair_cols(cum, h0)
            y = None
            for hx, keep in ((h0, lane_lo), (h0 + 1, ~lane_lo)):
                seg = col(cum, hx) - jnp.broadcast_to(cum_t[hx:hx + 1, :], (CHUNK, LANES))
                decay = jnp.where(causal, jnp.exp(jnp.where(causal, seg, 0.0)), 0.0)
                m = (cb * decay).astype(BF16)
                xm = jnp.where(keep, xdt, 0.0).astype(BF16)
                d = jnp.dot(m, xm, preferred_element_type=F32)
                y = d if y is None else y + d
            h_pair = h_ref[lanes, :]
            y_off = lax.dot_general(c_g, h_pair.astype(BF16), (((1,), (1,)), ((), ())),
                                    preferred_element_type=F32)
            ybuf_ref[:, lanes] = y + y_off * jnp.exp(cum_p)
            cum_end = cum_p[CHUNK - 1:CHUNK, :]
            x_end = (xdt * jnp.exp(cum_end - cum_p)).T.astype(BF16)
            upd = jnp.dot(x_end, b_g, preferred_element_type=F32)
            end0 = jnp.broadcast_to(cum[CHUNK - 1:CHUNK, h0:h0 + 1], (CHUNK, LANES))
            end1 = jnp.broadcast_to(cum[CHUNK - 1:CHUNK, h0 + 1:h0 + 2], (CHUNK, LANES))
            h_ref[lanes, :] = jnp.exp(jnp.where(row_lo, end0, end1)) * h_pair + upd

    zz = z_ref[0]
    yv = (ybuf_ref[0:rows, :] + xc_ref[0:rows, 0:SSD_INNER] * dskip_ref[...]) * (zz * jax.nn.sigmoid(zz))
    gw = SSD_INNER // SSD_GROUPS
    outs = []
    for g in range(SSD_GROUPS):
        yg = yv[:, gw * g:gw * (g + 1)]
        outs.append(yg * _rms_scale(yg))
    y_ref[0] = (jnp.concatenate(outs, axis=1) * nw_ref[...]).astype(BF16)

    @pl.when(c == pl.num_programs(1) - 1)
    def _():
        hout_ref[0] = h_ref[...]


def _ssd(xbc, z, dt_raw, h0, hist, conv_w, conv_b, dt_bias, a_neg, d_skip, norm_w, *, rows, first_valid):
    b, l, _ = xbc.shape
    hp = SSD_HEADS * SSD_HEAD_DIM
    blk = lambda bi, ci: (bi, ci, 0)
    per_b = lambda bi, ci: (bi, 0, 0)
    const = lambda bi, ci: (0, 0)
    return pl.pallas_call(
        functools.partial(_ssd_kernel, rows=rows, first_valid=first_valid),
        grid=(b, l // rows),
        in_specs=[
            pl.BlockSpec((1, rows, CONV_DIM), blk),
            pl.BlockSpec((1, rows, SSD_INNER), blk),
            pl.BlockSpec((1, rows, LANES), blk),
            pl.BlockSpec((1, hp, D_STATE), per_b),
            pl.BlockSpec((1, SUBLANES, CONV_DIM), per_b),
            pl.BlockSpec((CONV_W, CONV_DIM), const),
            pl.BlockSpec((1, CONV_DIM), const),
            pl.BlockSpec((1, LANES), const),
            pl.BlockSpec((1, LANES), const),
            pl.BlockSpec((1, SSD_INNER), const),
            pl.BlockSpec((1, SSD_INNER), const),
        ],
        out_specs=(pl.BlockSpec((1, rows, SSD_INNER), blk), pl.BlockSpec((1, hp, D_STATE), per_b)),
        out_shape=(jax.ShapeDtypeStruct((b, l, SSD_INNER), BF16),
                   jax.ShapeDtypeStruct((b, hp, D_STATE), F32)),
        scratch_shapes=[
            pltpu.VMEM((SUBLANES + CHUNK, CONV_DIM), F32),
            pltpu.VMEM((CHUNK, CONV_DIM), F32),
            pltpu.VMEM((CHUNK, LANES), F32),
            pltpu.VMEM((CHUNK, SSD_INNER), F32),
            pltpu.VMEM((hp, D_STATE), F32),
        ],
        compiler_params=_cparams(("arbitrary", "arbitrary")),
        name="ssd",
    )(xbc, z, dt_raw, h0, hist, conv_w, conv_b, dt_bias, a_neg, d_skip, norm_w)


def _outproj_kernel(x_ref, attn_ref, ssd_ref, anw_ref, woa_ref, wos_ref, fnw_ref,
                    rwh_ref, rwl_ref, rb_ref, x2_ref, xn_ref, comb_ref):
    a = attn_ref[...]
    an = ((a * _rms_scale(a)) * anw_ref[...]).astype(BF16)
    d = jnp.dot(an, woa_ref[...], preferred_element_type=F32)
    d = d + jnp.dot(ssd_ref[...], wos_ref[...], preferred_element_type=F32)
    x2 = x_ref[...] + d
    x2_ref[...] = x2
    xn = (x2 * _rms_scale(x2)) * fnw_ref[...]
    xh = xn.astype(BF16)
    xl = (xn - xh.astype(F32)).astype(BF16)
    xn_ref[...] = xh

    logits = (jnp.dot(xh, rwh_ref[...], preferred_element_type=F32)
              + jnp.dot(xh, rwl_ref[...], preferred_element_type=F32)
              + jnp.dot(xl, rwh_ref[...], preferred_element_type=F32)) + rb_ref[...]
    lane = lax.broadcasted_iota(jnp.int32, logits.shape, 1)
    lane_f = lane.astype(F32)
    ninf = jnp.float32(-jnp.inf)
    far = jnp.float32(2 * LANES)

    def first_max(vals):
        m = jnp.max(vals, axis=1, keepdims=True)
        idx = jnp.min(jnp.where(vals == m, lane_f, far), axis=1, keepdims=True)
        return m, idx

    g_mask = (lane >= N_EXPERTS) & (lane < N_EXPERTS + N_EXPERT_GROUPS)
    gl = jnp.where(g_mask, logits, ninf)
    g_max, g_idx = first_max(gl)
    g_w = 1.0 / jnp.sum(jnp.where(g_mask, jnp.exp(gl - g_max), 0.0), axis=1, keepdims=True)
    g_sel = g_idx - N_EXPERTS
    e_mask = (lane < N_EXPERTS) & ((lane // EXPERTS_PER_GROUP).astype(F32) == g_sel)
    el = jnp.where(e_mask, logits, ninf)
    m1, i1 = first_max(el)
    m2, i2 = first_max(jnp.where(lane_f == i1, ninf, el))
    ex = jnp.exp(m2 - m1)
    w1 = g_w / (1.0 + ex)
    w2 = g_w * (ex / (1.0 + ex))
    comb_ref[...] = jnp.where(lane_f == i1, w1, jnp.where(lane_f == i2, w2, 0.0))


def _out_proj(x, attn, ssd, attn_norm_w, wo_a, wo_s, ffn_norm_w, rw_hi, rw_lo, rb, *, tm, row0):
    n = attn.shape[0] - row0 * tm
    rows_in = lambda i: (i + row0, 0)
    rows_out = lambda i: (i, 0)
    const = lambda i: (0, 0)
    return pl.pallas_call(
        _outproj_kernel,
        grid=(n // tm,),
        in_specs=[
            pl.BlockSpec((tm, D_MODEL), rows_in),
            pl.BlockSpec((tm, ATTN_WIDTH), rows_in),
            pl.BlockSpec((tm, SSD_INNER), rows_in),
            pl.BlockSpec((1, ATTN_WIDTH), const),
            pl.BlockSpec((ATTN_WIDTH, D_MODEL), const),
            pl.BlockSpec((SSD_INNER, D_MODEL), const),
            pl.BlockSpec((1, D_MODEL), const),
            pl.BlockSpec((D_MODEL, LANES), const),
            pl.BlockSpec((D_MODEL, LANES), const),
            pl.BlockSpec((1, LANES), const),
        ],
        out_specs=(pl.BlockSpec((tm, D_MODEL), rows_out), pl.BlockSpec((tm, D_MODEL), rows_out),
                   pl.BlockSpec((tm, LANES), rows_out)),
        out_shape=(jax.ShapeDtypeStruct((n, D_MODEL), F32), jax.ShapeDtypeStruct((n, D_MODEL), BF16),
                   jax.ShapeDtypeStruct((n, LANES), F32)),
        compiler_params=_cparams(("arbitrary",)),
        name="out_proj_router",
    )(x, attn, ssd, attn_norm_w, wo_a, wo_s, ffn_norm_w, rw_hi, rw_lo, rb)


def _moe_kernel(xn_ref, comb_ref, x2_ref, wg_ref, wu_ref, wd_ref, fw_ref, y_ref, acc_ref):
    e = pl.program_id(1)

    @pl.when(e == 0)
    def _():
        acc_ref[...] = jnp.zeros_like(acc_ref)

    x = xn_ref[...]
    g = jnp.dot(x, wg_ref[0], preferred_element_type=F32)
    u = jnp.dot(x, wu_ref[0], preferred_element_type=F32)
    comb = comb_ref[...]
    lane = lax.broadcasted_iota(jnp.int32, comb.shape, 1)
    c = jnp.sum(jnp.where(lane == e, comb, 0.0), axis=1, keepdims=True)
    h = ((g * jax.nn.sigmoid(g)) * u * c).astype(BF16)
    acc_ref[...] += jnp.dot(h, wd_ref[0], preferred_element_type=F32)

    @pl.when(e == pl.num_programs(1) - 1)
    def _():
        y = x2_ref[...] + acc_ref[...]
        y_ref[...] = (y * _rms_scale(y)) * fw_ref[...]


def _moe(xn, comb, x2, w_gate, w_up, w_down, final_w, *, tm):
    n = xn.shape[0]
    rows = lambda i, e: (i, 0)
    per_e = lambda i, e: (e, 0, 0)
    return pl.pallas_call(
        _moe_kernel,
        grid=(n // tm, N_EXPERTS),
        in_specs=[
            pl.BlockSpec((tm, D_MODEL), rows),
            pl.BlockSpec((tm, LANES), rows),
            pl.BlockSpec((tm, D_MODEL), rows),
            pl.BlockSpec((1, D_MODEL, D_EXPERT), per_e),
            pl.BlockSpec((1, D_MODEL, D_EXPERT), per_e),
            pl.BlockSpec((1, D_EXPERT, D_MODEL), per_e),
            pl.BlockSpec((1, D_MODEL), lambda i, e: (0, 0)),
        ],
        out_specs=pl.BlockSpec((tm, D_MODEL), rows),
        out_shape=jax.ShapeDtypeStruct((n, D_MODEL), F32),
        scratch_shapes=[pltpu.VMEM((tm, D_MODEL), F32)],
        compiler_params=_cparams(("arbitrary", "arbitrary")),
        name="moe",
    )(xn, comb, x2, w_gate, w_up, w_down, final_w)


def _largest_tile(n, cap):
    best = LANES
    for m in range(LANES, cap + 1, LANES):
        if n % m == 0:
            best = m
    return best


def _permute_key_tiles(x):
    t = x.shape[0]
    x = x.reshape(t // KEY_TILE, SUBLANES, STREAM, HEAD_PAIRS, 2 * ATTN_HEAD_DIM)
    return x.transpose(3, 0, 2, 1, 4).reshape(HEAD_PAIRS, t // KEY_TILE, KEY_TILE, 2 * ATTN_HEAD_DIM)


def kernel(x_prompt, x_sample, cache_k, cache_v, page_table, state_ssm, state_conv, meta_tokens,
           norm_mix_w, w_in, attn_logit_bias, conv_w, conv_b, dt_bias, a_log, d_skip, ssd_norm_w,
           attn_norm_w, w_out, norm_ffn_w, router_group_w, router_group_b, router_expert_w,
           router_expert_b, expert_w_gate, expert_w_up, expert_w_down, final_norm_w):
    assert w_in.shape[0] == 1 and x_prompt.shape[0] == 1
    seq = x_prompt.shape[1]
    dec_batch, dec_seq, _ = x_sample.shape
    assert seq % Q_BLOCK == 0 and dec_seq == SUBLANES

    w_in0 = w_in[0]
    n_main = 3 * ATTN_WIDTH + SSD_INNER + CONV_DIM
    w_main = w_in0[:, :n_main].astype(BF16)
    w_dt = jnp.pad(w_in0[:, n_main:], ((0, 0), (0, LANES - SSD_HEADS))).astype(BF16)
    nmw = norm_mix_w[0][None]
    pad_heads = lambda v: jnp.pad(v, (0, LANES - SSD_HEADS))[None]
    dtb = pad_heads(dt_bias[0])
    a_neg = pad_heads(-jnp.exp(a_log[0]))
    dskip = jnp.repeat(d_skip[0], SSD_HEAD_DIM)[None]
    ssd_nw = ssd_norm_w[0][None]
    conv_w0, conv_b0 = conv_w[0], conv_b[0][None]
    wo = w_out[0].astype(BF16)
    wo_a, wo_s = wo[:ATTN_WIDTH], wo[ATTN_WIDTH:]
    rw = jnp.concatenate([router_expert_w[0], router_group_w[0]], axis=1)
    rw = jnp.pad(rw, ((0, 0), (0, LANES - rw.shape[1])))
    rw_hi = rw.astype(BF16)
    rw_lo = (rw - rw_hi.astype(F32)).astype(BF16)
    rb = jnp.concatenate([router_expert_b[0], router_group_b[0]])
    rb = jnp.pad(rb, (0, LANES - rb.shape[0]))[None]
    wg, wu, wd = (w[0].astype(BF16) for w in (expert_w_gate, expert_w_up, expert_w_down))
    anw, fnw, finw = attn_norm_w[0][None], norm_ffn_w[0][None], final_norm_w[None]
    bias = attn_logit_bias[0]

    def tail(x, attn, ssd, tm, row0):
        x2, xn, comb = _out_proj(x, attn, ssd, anw, wo_a, wo_s, fnw, rw_hi, rw_lo, rb, tm=tm, row0=row0)
        return _moe(xn, comb, x2, wg, wu, wd, finw, tm=_largest_tile(xn.shape[0], 512))

    xp = jnp.concatenate([jnp.zeros((PROMPT_PAD, D_MODEL), F32), meta_tokens.astype(F32), x_prompt[0]], axis=0)
    tp = xp.shape[0]
    q_p, k32_p, v32_p, kb_p, vb_p, z_p, xbc_p, dt_p = _in_proj(xp, nmw, w_main, w_dt, _largest_tile(tp, 640))
    k_perm = _permute_key_tiles(kb_p.reshape(tp, HEAD_PAIRS, 2 * ATTN_HEAD_DIM))
    v_t = _permute_key_tiles(vb_p.reshape(tp, HEAD_PAIRS, 2 * ATTN_HEAD_DIM)).transpose(0, 1, 3, 2)
    attn_p = _prompt_attention(bias, q_p.T, k_perm, v_t)
    hp = SSD_HEADS * SSD_HEAD_DIM
    ssd_p, h_p = _ssd(xbc_p[None], z_p[None], dt_p[None],
                      jnp.zeros((1, hp, D_STATE), F32), jnp.zeros((1, SUBLANES, CONV_DIM), F32),
                      conv_w0, conv_b0, dtb, a_neg, dskip, ssd_nw, rows=CHUNK, first_valid=PROMPT_PAD)
    y_prompt = tail(xp, attn_p, ssd_p[0], Q_BLOCK, 1)

    n_s = dec_batch * dec_seq
    xs = x_sample.reshape(n_s, D_MODEL)
    q_s, k32_s, v32_s, _, _, z_s, xbc_s, dt_s = _in_proj(xs, nmw, w_main, w_dt, _largest_tile(n_s, 512))
    q4 = q_s.reshape(dec_batch, dec_seq, ATTN_HEADS, ATTN_HEAD_DIM)
    eye = jnp.eye(ATTN_HEADS, dtype=BF16)
    q_bd = q4.transpose(0, 2, 1, 3)[:, :, :, None, :] * eye[None, :, None, :, None]
    q_bd = q_bd.reshape(dec_batch, ATTN_HEADS * dec_seq, ATTN_WIDTH)
    bias_rows = jnp.broadcast_to(jnp.repeat(bias, dec_seq)[:, None], (ATTN_HEADS * dec_seq, PAGE_SIZE))

    def pages_t(x):
        return jnp.moveaxis(x, -3, -1)

    def spread(x):
        x = pages_t(x.reshape(dec_batch, dec_seq, ATTN_HEADS, ATTN_HEAD_DIM))
        return jnp.pad(x, ((0, 0), (0, 0), (0, 0), (0, PAGE_SIZE - dec_seq)))

    attn_s = _sample_attention(page_table, q_bd, bias_rows, spread(k32_s), spread(v32_s),
                               pages_t(cache_k[0]), pages_t(cache_v[0]), dec_seq)
    hist_s = jnp.pad(state_conv[0], ((0, 0), (SUBLANES - (CONV_W - 1), 0), (0, 0)))
    ssd_s, h_s = _ssd(xbc_s.reshape(dec_batch, dec_seq, CONV_DIM), z_s.reshape(dec_batch, dec_seq, SSD_INNER),
                      dt_s.reshape(dec_batch, dec_seq, LANES),
                      state_ssm[0].reshape(dec_batch, hp, D_STATE), hist_s,
                      conv_w0, conv_b0, dtb, a_neg, dskip, ssd_nw, rows=dec_seq, first_valid=0)
    y_sample = tail(xs, attn_s.reshape(n_s, ATTN_WIDTH), ssd_s.reshape(n_s, SSD_INNER), _largest_tile(n_s, 256), 0)

    heads = (ATTN_HEADS, ATTN_HEAD_DIM)
    state = (SSD_HEADS, SSD_HEAD_DIM, D_STATE)
    xbc_s3 = xbc_s.reshape(dec_batch, dec_seq, CONV_DIM)
    return (
        y_prompt[None],
        y_sample.reshape(dec_batch, dec_seq, D_MODEL),
        k32_p[PROMPT_PAD:].reshape(1, 1, tp - PROMPT_PAD, *heads),
        v32_p[PROMPT_PAD:].reshape(1, 1, tp - PROMPT_PAD, *heads),
        h_p.reshape(1, 1, *state).astype(state_ssm.dtype),
        xbc_p[tp - (CONV_W - 1):][None, None],
        k32_s.reshape(1, dec_batch, dec_seq, *heads),
        v32_s.reshape(1, dec_batch, dec_seq, *heads),
        h_s.reshape(1, dec_batch, *state).astype(state_ssm.dtype),
        xbc_s3[:, dec_seq - (CONV_W - 1):][None],
    )
```

```python
import functools

import jax
import jax.numpy as jnp
from jax import lax
from jax.experimental import pallas as pl
from jax.experimental.pallas import tpu as pltpu

F32 = jnp.float32
BF16 = jnp.bfloat16

LANES = 128
SUBLANES = 8
VMEM_LIMIT = 56 * 1024 * 1024

N_META = 16
D_MODEL = 1024
ATTN_HEADS = 16
ATTN_HEAD_DIM = 64
ATTN_WIDTH = ATTN_HEADS * ATTN_HEAD_DIM
ATTN_SCALE = ATTN_HEAD_DIM ** -0.5
HEAD_PAIRS = ATTN_HEADS // 2
SSD_INNER = 2 * D_MODEL
SSD_HEAD_DIM = 64
SSD_HEADS = SSD_INNER // SSD_HEAD_DIM
SSD_GROUPS = 4
D_STATE = 128
CONV_W = 4
GROUP_BC = SSD_GROUPS * D_STATE
CONV_DIM = SSD_INNER + 2 * GROUP_BC
N_EXPERT_GROUPS = 4
EXPERTS_PER_GROUP = 8
N_EXPERTS = N_EXPERT_GROUPS * EXPERTS_PER_GROUP
D_EXPERT = 512
PAGE_SIZE = 128
EPS = 1e-6

Q_BLOCK = 256
KEY_TILE = 128
STREAM = KEY_TILE // SUBLANES
TILES_PER_STEP = Q_BLOCK // KEY_TILE
CHUNK = 128
PROMPT_PAD = Q_BLOCK - N_META
PAGES_PER_STEP = 8


def _rms_scale(x):
    return lax.rsqrt(jnp.mean(x * x, axis=-1, keepdims=True) + EPS)


def _cparams(sem):
    return pltpu.CompilerParams(dimension_semantics=sem, vmem_limit_bytes=VMEM_LIMIT)


def _inproj_kernel(x_ref, nw_ref, w_ref, wdt_ref,
                   q_ref, k32_ref, v32_ref, kb_ref, vb_ref, z_ref, xbc_ref, dt_ref, xn_ref):
    j = pl.program_id(1)

    @pl.when(j == 0)
    def _():
        x = x_ref[...]
        xn_ref[...] = ((x * _rms_scale(x)) * nw_ref[...]).astype(BF16)

    res = jnp.dot(xn_ref[...], w_ref[...], preferred_element_type=F32)

    @pl.when(j == 0)
    def _():
        q_ref[...] = (res * ATTN_SCALE).astype(BF16)

    @pl.when(j == 1)
    def _():
        k32_ref[...] = res
        kb_ref[...] = res.astype(BF16)

    @pl.when(j == 2)
    def _():
        v32_ref[...] = res
        vb_ref[...] = res.astype(BF16)

    @pl.when((j == 3) | (j == 4))
    def _():
        z_ref[...] = res

    @pl.when(j >= 5)
    def _():
        xbc_ref[...] = res

    @pl.when(j == 7)
    def _():
        dt_ref[...] = jnp.dot(xn_ref[...], wdt_ref[...], preferred_element_type=F32)


def _in_proj(x, norm_w, w_main, w_dt, tm):
    t = x.shape[0]
    nb = D_MODEL
    n_col = w_main.shape[1] // nb
    row = lambda i, j: (i, 0)
    out_shape = (
        jax.ShapeDtypeStruct((t, ATTN_WIDTH), BF16),
        jax.ShapeDtypeStruct((t, ATTN_WIDTH), F32),
        jax.ShapeDtypeStruct((t, ATTN_WIDTH), F32),
        jax.ShapeDtypeStruct((t, ATTN_WIDTH), BF16),
        jax.ShapeDtypeStruct((t, ATTN_WIDTH), BF16),
        jax.ShapeDtypeStruct((t, SSD_INNER), F32),
        jax.ShapeDtypeStruct((t, CONV_DIM), F32),
        jax.ShapeDtypeStruct((t, LANES), F32),
    )
    out_specs = (
        pl.BlockSpec((tm, nb), row), pl.BlockSpec((tm, nb), row), pl.BlockSpec((tm, nb), row),
        pl.BlockSpec((tm, nb), row), pl.BlockSpec((tm, nb), row),
        pl.BlockSpec((tm, nb), lambda i, j: (i, jnp.clip(j - 3, 0, 1))),
        pl.BlockSpec((tm, nb), lambda i, j: (i, jnp.clip(j - 5, 0, 2))),
        pl.BlockSpec((tm, LANES), row),
    )
    return pl.pallas_call(
        _inproj_kernel,
        grid=(t // tm, n_col),
        in_specs=[
            pl.BlockSpec((tm, D_MODEL), row),
            pl.BlockSpec((1, D_MODEL), lambda i, j: (0, 0)),
            pl.BlockSpec((D_MODEL, nb), lambda i, j: (0, j)),
            pl.BlockSpec((D_MODEL, LANES), lambda i, j: (0, 0)),
        ],
        out_specs=out_specs,
        out_shape=out_shape,
        scratch_shapes=[pltpu.VMEM((tm, D_MODEL), BF16)],
        compiler_params=_cparams(("arbitrary", "arbitrary")),
        name="in_proj",
    )(x, norm_w, w_main, w_dt)


def _key_in_tile(v, n):
    return STREAM * lax.broadcasted_iota(jnp.int32, (SUBLANES, n), 0) + v


def _sublane_shift_up(x, k):
    sub = lax.broadcasted_iota(jnp.int32, x.shape, 0)
    return jnp.where(sub < SUBLANES - k, pltpu.roll(x, SUBLANES - k, 0), 0.0)


def _sublane_suffix(tot):
    s = tot
    for k in (1, 2, 4):
        s = s + _sublane_shift_up(s, k)
    return s


def _softplus(z):
    sign = jnp.uint32(0x80000000)
    neg_abs = lax.bitcast_convert_type(lax.bitcast_convert_type(z, jnp.uint32) | sign, F32)
    return jnp.maximum(z, 0.0) + jnp.log(1.0 + jnp.exp(neg_abs))


def _sb_tile(logits, bias, carry, valid):
    nv = KEY_TILE // SUBLANES
    keep = jnp.ones_like(carry)
    part = [None] * nv
    for v in range(nv - 1, -1, -1):
        beta = jax.nn.sigmoid(logits(v) + bias)
        if valid is not None:
            beta = jnp.where(valid(v), beta, 0.0)
        part[v] = beta * keep
        keep = keep - part[v]
        if v % 2 == 0:
            part[v] = jnp.concatenate([part[v], part[v + 1]], axis=0).astype(BF16)
            part[v + 1] = None
    drop = -jnp.log(keep)
    later = _sublane_suffix(_sublane_shift_up(drop, 1))
    scale = jnp.exp(-(later + carry))
    new_carry = carry + jnp.broadcast_to((later + drop)[0:1], carry.shape)
    scale2 = jnp.concatenate([scale, scale], axis=0).astype(BF16)
    return jnp.concatenate([part[v] * scale2 for v in range(0, nv, 2)], axis=0), new_carry


def _prompt_attn_kernel(bias_ref, qt_ref, k_ref, vt_ref, o_ref, acc_ref, carry_ref, *sw_refs):
    p = pl.program_id(0)
    i = pl.program_id(1)
    n_groups = k_ref.shape[1] // TILES_PER_STEP
    n_col = Q_BLOCK // LANES
    w_refs = sw_refs[0:2]
    s_refs = [[sw_refs[2 + n_col * (2 * u + hh):2 + n_col * (2 * u + hh + 1)] for hh in range(2)]
              for u in range(TILES_PER_STEP)]
    q2 = qt_ref[...]
    row = lax.broadcasted_iota(jnp.int32, q2.shape, 0)
    qpos = i * Q_BLOCK + lax.broadcasted_iota(jnp.int32, (SUBLANES, LANES), 1)
    qpads = [jnp.where((row // ATTN_HEAD_DIM) == hh, q2, jnp.zeros_like(q2)) for hh in range(2)]
    biases = [bias_ref[2 * p + hh] for hh in range(2)]
    hrows = [slice(ATTN_HEAD_DIM * hh, ATTN_HEAD_DIM * (hh + 1)) for hh in range(2)]

    def tile_of(g, u):
        return TILES_PER_STEP * g + (TILES_PER_STEP - 1 - u)

    def logits(g, u, hh):
        return jnp.dot(k_ref[0, tile_of(g, u)], qpads[hh], preferred_element_type=F32)

    def store_logits(u, hh, s):
        for c in range(n_col):
            s_refs[u][hh][c][...] = s[:, LANES * c:LANES * (c + 1)]

    def values(g, hh):
        v_t = jnp.concatenate([vt_ref[0, tile_of(g, u), hrows[hh], :] for u in range(TILES_PER_STEP)], axis=1)
        return jnp.dot(v_t, w_refs[hh][...], preferred_element_type=F32)

    def step(g_prev, g_cur, g_next, masked):
        pv = [values(g_prev, hh) for hh in range(2)]
        s_next = {(u, hh): logits(g_next, u, hh) for u in range(TILES_PER_STEP) for hh in range(2)}
        ws = {}
        for u in range(TILES_PER_STEP):
            for hh in range(2):
                halves = []
                for c in range(n_col):
                    valid = None
                    if masked:
                        def valid(v, u=u, c=c):
                            kpos = tile_of(g_cur, u) * KEY_TILE + _key_in_tile(v, LANES)
                            return (kpos < qpos + LANES * c) & (kpos >= PROMPT_PAD)
                    def slab(v, u=u, hh=hh, c=c):
                        return s_refs[u][hh][c][SUBLANES * v:SUBLANES * (v + 1), :]
                    w, nc = _sb_tile(slab, biases[hh], carry_ref[hh, c], valid)
                    carry_ref[hh, c] = nc
                    halves.append(w)
                ws[u, hh] = jnp.concatenate(halves, axis=1).astype(BF16)
        for hh in range(2):
            for u in range(TILES_PER_STEP):
                w_refs[hh][KEY_TILE * u:KEY_TILE * (u + 1), :] = ws[u, hh]
                store_logits(u, hh, s_next[u, hh])
            acc_ref[hrows[hh], :] += pv[hh]

    acc_ref[...] = jnp.zeros_like(acc_ref)
    carry_ref[...] = jnp.zeros_like(carry_ref)
    for hh in range(2):
        for u in range(TILES_PER_STEP):
            store_logits(u, hh, logits(i, u, hh))
        w_refs[hh][...] = jnp.zeros_like(w_refs[hh])

    @pl.when(i > 0)
    def _():
        step(jnp.minimum(i + 1, n_groups - 1), i, i - 1, True)

        def full(n, c):
            g = i - 1 - 2 * n
            step(g + 1, g, g - 1, False)
            step(g, g - 1, g - 2, False)
            return c
        lax.fori_loop(0, (i - 1) // 2, full, 0)

        @pl.when((i - 1) % 2 == 1)
        def _():
            step(2, 1, 0, False)

    step(1, 0, 0, True)
    for hh in range(2):
        acc_ref[hrows[hh], :] += values(0, hh)
    o_ref[...] = acc_ref[...].T


def _prompt_attention(bias, q_t, k_perm, v_t):
    t = q_t.shape[1]
    n_kt = t // KEY_TILE
    pair_w = 2 * ATTN_HEAD_DIM
    return pl.pallas_call(
        _prompt_attn_kernel,
        grid=(HEAD_PAIRS, t // Q_BLOCK),
        in_specs=[
            pl.BlockSpec(memory_space=pltpu.SMEM),
            pl.BlockSpec((pair_w, Q_BLOCK), lambda p, i: (p, i)),
            pl.BlockSpec((1, n_kt, KEY_TILE, pair_w), lambda p, i: (p, 0, 0, 0)),
            pl.BlockSpec((1, n_kt, pair_w, KEY_TILE), lambda p, i: (p, 0, 0, 0)),
        ],
        out_specs=pl.BlockSpec((Q_BLOCK, pair_w), lambda p, i: (i, p)),
        out_shape=jax.ShapeDtypeStruct((t, ATTN_WIDTH), F32),
        scratch_shapes=[pltpu.VMEM((pair_w, Q_BLOCK), F32),
                        pltpu.VMEM((2, Q_BLOCK // LANES, SUBLANES, LANES), F32)]
        + [pltpu.VMEM((TILES_PER_STEP * KEY_TILE, Q_BLOCK), BF16)] * 2
        + [pltpu.VMEM((KEY_TILE, LANES), F32)] * (2 * TILES_PER_STEP * (Q_BLOCK // LANES)),
        compiler_params=_cparams(("arbitrary", "arbitrary")),
        name="prompt_attention",
    )(bias, q_t, k_perm, v_t)


def _sample_attn_kernel(pt_ref, qbd_ref, bias_ref, knew_ref, vnew_ref, *rest, dec_seq):
    n_pg = PAGES_PER_STEP
    k_refs = rest[:n_pg]
    v_refs = rest[n_pg:2 * n_pg]
    o_ref, acc_ref, carry_ref = rest[2 * n_pg:]
    s_id = pl.program_id(1)
    qbd = qbd_ref[0]
    bias = bias_ref[...]
    r_io = lax.broadcasted_iota(jnp.int32, (LANES, PAGE_SIZE), 0)
    c_io = lax.broadcasted_iota(jnp.int32, (LANES, PAGE_SIZE), 1)

    def rows_of(ref):
        return ref[...].reshape(ATTN_WIDTH, PAGE_SIZE).astype(BF16)

    def tile(kv_refs, valid):
        n_keys = PAGE_SIZE * len(kv_refs)
        k_t = jnp.concatenate([rows_of(k) for k, _ in kv_refs], axis=1)
        wide = lambda x: jnp.concatenate([x] * len(kv_refs), axis=1)
        z = jnp.dot(qbd, k_t, preferred_element_type=F32) + wide(bias)
        c = _softplus(z)
        if valid is not None:
            c = jnp.where(valid, c, 0.0)
        j_io = lax.broadcasted_iota(jnp.int32, (n_keys, n_keys), 0)
        k_io = lax.broadcasted_iota(jnp.int32, (n_keys, n_keys), 1)
        later_or_same = (j_io >= k_io).astype(BF16)
        incl = sum(jnp.dot(part, later_or_same, preferred_element_type=F32) for part in _split3(c))
        carry = carry_ref[...]
        w = jnp.exp(z - (incl + wide(carry)))
        if valid is not None:
            w = jnp.where(valid, w, 0.0)
        carry_ref[...] = carry + jnp.broadcast_to(incl[:, 0:1], carry.shape)
        v_t = jnp.concatenate([rows_of(v) for _, v in kv_refs], axis=1)
        return v_t, w.T.astype(BF16)

    def accumulate(tiles):
        v_all = jnp.concatenate([v for v, _ in tiles], axis=1)
        w_all = jnp.concatenate([w for _, w in tiles], axis=0)
        acc_ref[...] += jnp.dot(v_all, w_all, preferred_element_type=F32)

    @pl.when(s_id == 0)
    def _():
        acc_ref[...] = jnp.zeros_like(acc_ref)
        carry_ref[...] = jnp.zeros_like(carry_ref)
        accumulate([tile([(knew_ref, vnew_ref)], c_io < r_io % dec_seq)])

    accumulate([tile([(k_refs[r + 1], v_refs[r + 1]), (k_refs[r], v_refs[r])], None)
                for r in range(0, n_pg, 2)])

    @pl.when(s_id == pl.num_programs(1) - 1)
    def _():
        row = lax.broadcasted_iota(jnp.int32, acc_ref.shape, 0)
        lane = lax.broadcasted_iota(jnp.int32, acc_ref.shape, 1)
        own = jnp.where(row // ATTN_HEAD_DIM == lane // dec_seq, acc_ref[...], 0.0)
        fold = (r_io % dec_seq == c_io).astype(BF16)
        out_t = sum(jnp.dot(part, fold, preferred_element_type=F32) for part in _split3(own))
        o_ref[0] = out_t.T[0:dec_seq, :]


def _sample_attention(page_table, q_bd, bias_rows, k_new_t, v_new_t, cache_k_t, cache_v_t, dec_seq):
    b, n_pages = page_table.shape
    n_steps = n_pages // PAGES_PER_STEP
    page_block = (None, ATTN_HEADS, ATTN_HEAD_DIM, PAGE_SIZE)

    def page_spec(r):
        def idx(bi, si, pt):
            return (pt[bi, n_pages - 1 - (si * PAGES_PER_STEP + r)], 0, 0, 0)
        return pl.BlockSpec(page_block, idx)

    per_seq = lambda bi, si, pt: (bi, 0, 0)
    per_seq4 = lambda bi, si, pt: (bi, 0, 0, 0)
    grid_spec = pltpu.PrefetchScalarGridSpec(
        num_scalar_prefetch=1,
        grid=(b, n_steps),
        in_specs=[
            pl.BlockSpec((1, LANES, ATTN_WIDTH), per_seq),
            pl.BlockSpec((LANES, PAGE_SIZE), lambda bi, si, pt: (0, 0)),
            pl.BlockSpec(page_block, per_seq4),
            pl.BlockSpec(page_block, per_seq4),
        ] + [page_spec(r) for r in range(PAGES_PER_STEP)] * 2,
        out_specs=pl.BlockSpec((1, dec_seq, ATTN_WIDTH), per_seq),
        scratch_shapes=[pltpu.VMEM((ATTN_WIDTH, LANES), F32), pltpu.VMEM((LANES, PAGE_SIZE), F32)],
    )
    return pl.pallas_call(
        functools.partial(_sample_attn_kernel, dec_seq=dec_seq),
        grid_spec=grid_spec,
        out_shape=jax.ShapeDtypeStruct((b, dec_seq, ATTN_WIDTH), F32),
        compiler_params=_cparams(("arbitrary", "arbitrary")),
        name="sample_attention",
    )(page_table, q_bd, bias_rows, k_new_t, v_new_t,
      *([cache_k_t] * PAGES_PER_STEP), *([cache_v_t] * PAGES_PER_STEP))


def _split3(x):
    hi = x.astype(BF16)
    r1 = x - hi.astype(F32)
    mid = r1.astype(BF16)
    lo = (r1 - mid.astype(F32)).astype(BF16)
    return hi, mid, lo


def _ssd_kernel(xbc_ref, z_ref, dt_ref, h0_ref, hist_ref, convw_ref, convb_ref, dtb_ref, a_ref,
                dskip_ref, nw_ref, y_ref, hout_ref, buf_ref, xc_ref, dtbuf_ref, ybuf_ref, h_ref,
                *, rows, first_valid):
    c = pl.program_id(1)
    hist = SUBLANES

    @pl.when(c == 0)
    def _():
        h_ref[...] = h0_ref[0]
        buf_ref[...] = jnp.zeros_like(buf_ref)
        buf_ref[0:hist, :] = hist_ref[0]
        dtbuf_ref[...] = jnp.zeros_like(dtbuf_ref)

    buf_ref[hist:hist + rows, :] = xbc_ref[0]
    dtbuf_ref[0:rows, :] = dt_ref[0]

    conv = convb_ref[...]
    for tap in range(CONV_W):
        start = hist - (CONV_W - 1) + tap
        conv = conv + buf_ref[start:start + CHUNK, :] * convw_ref[tap:tap + 1, :]
    xc_ref[...] = conv * jax.nn.sigmoid(conv)
    buf_ref[0:hist, :] = buf_ref[rows:rows + hist, :]

    t_io = lax.broadcasted_iota(jnp.int32, (CHUNK, LANES), 0)
    s_io = lax.broadcasted_iota(jnp.int32, (CHUNK, LANES), 1)
    x_dt = dtbuf_ref[...] + dtb_ref[...]
    dt = jnp.maximum(x_dt, 0.0) + jnp.log1p(jnp.exp(-jnp.abs(x_dt)))
    live = (t_io < rows) & (c * CHUNK + t_io >= first_valid)
    dt = jnp.where(live, dt, 0.0)
    la = dt * a_ref[...]
    tril = (s_io <= t_io).astype(BF16)
    cum = sum(jnp.dot(tril, part, preferred_element_type=F32) for part in _split3(la))
    cum_t = cum.T
    causal = s_io <= t_io
    lane_lo = s_io < SSD_HEAD_DIM
    row_lo = t_io < SSD_HEAD_DIM

    def col(x, h):
        return jnp.broadcast_to(x[:, h:h + 1], (CHUNK, LANES))

    def pair_cols(x, h0):
        return jnp.where(lane_lo, col(x, h0), col(x, h0 + 1))

    heads_per_group = SSD_HEADS // SSD_GROUPS
    for g in range(SSD_GROUPS):
        b_g = xc_ref[:, SSD_INNER + D_STATE * g:SSD_INNER + D_STATE * (g + 1)].astype(BF16)
        c_g = xc_ref[:, SSD_INNER + GROUP_BC + D_STATE * g:
                     SSD_INNER + GROUP_BC + D_STATE * (g + 1)].astype(BF16)
        cb = lax.dot_general(c_g, b_g, (((1,), (1,)), ((), ())), preferred_element_type=F32)
        for pr in range(heads_per_group // 2):
            h0 = heads_per_group * g + 2 * pr
            lanes = slice(SSD_HEAD_DIM * h0, SSD_HEAD_DIM * (h0 + 2))
            xdt = xc_ref[:, lanes] * pair_cols(dt, h0)
            cum_p = pair_cols(cum, h0)
            y = None
            for hx, keep in ((h0, lane_lo), (h0 + 1, ~lane_lo)):
                seg = col(cum, hx) - jnp.broadcast_to(cum_t[hx:hx + 1, :], (CHUNK, LANES))
                decay = jnp.where(causal, jnp.exp(jnp.where(causal, seg, 0.0)), 0.0)
                m = (cb * decay).astype(BF16)
                xm = jnp.where(keep, xdt, 0.0).astype(BF16)
                d = jnp.dot(m, xm, preferred_element_type=F32)
                y = d if y is None else y + d
            h_pair = h_ref[lanes, :]
            y_off = lax.dot_general(c_g, h_pair.astype(BF16), (((1,), (1,)), ((), ())),
                                    preferred_element_type=F32)
            ybuf_ref[:, lanes] = y + y_off * jnp.exp(cum_p)
            cum_end = cum_p[CHUNK - 1:CHUNK, :]
            x_end = (xdt * jnp.exp(cum_end - cum_p)).T.astype(BF16)
            upd = jnp.dot(x_end, b_g, preferred_element_type=F32)
            end0 = jnp.broadcast_to(cum[CHUNK - 1:CHUNK, h0:h0 + 1], (CHUNK, LANES))
            end1 = jnp.broadcast_to(cum[CHUNK - 1:CHUNK, h0 + 1:h0 + 2], (CHUNK, LANES))
            h_ref[lanes, :] = jnp.exp(jnp.where(row_lo, end0, end1)) * h_pair + upd

    zz = z_ref[0]
    yv = (ybuf_ref[0:rows, :] + xc_ref[0:rows, 0:SSD_INNER] * dskip_ref[...]) * (zz * jax.nn.sigmoid(zz))
    gw = SSD_INNER // SSD_GROUPS
    outs = []
    for g in range(SSD_GROUPS):
        yg = yv[:, gw * g:gw * (g + 1)]
        outs.append(yg * _rms_scale(yg))
    y_ref[0] = (jnp.concatenate(outs, axis=1) * nw_ref[...]).astype(BF16)

    @pl.when(c == pl.num_programs(1) - 1)
    def _():
        hout_ref[0] = h_ref[...]


def _ssd(xbc, z, dt_raw, h0, hist, conv_w, conv_b, dt_bias, a_neg, d_skip, norm_w, *, rows, first_valid):
    b, l, _ = xbc.shape
    hp = SSD_HEADS * SSD_HEAD_DIM
    blk = lambda bi, ci: (bi, ci, 0)
    per_b = lambda bi, ci: (bi, 0, 0)
    const = lambda bi, ci: (0, 0)
    return pl.pallas_call(
        functools.partial(_ssd_kernel, rows=rows, first_valid=first_valid),
        grid=(b, l // rows),
        in_specs=[
            pl.BlockSpec((1, rows, CONV_DIM), blk),
            pl.BlockSpec((1, rows, SSD_INNER), blk),
            pl.BlockSpec((1, rows, LANES), blk),
            pl.BlockSpec((1, hp, D_STATE), per_b),
            pl.BlockSpec((1, SUBLANES, CONV_DIM), per_b),
            pl.BlockSpec((CONV_W, CONV_DIM), const),
            pl.BlockSpec((1, CONV_DIM), const),
            pl.BlockSpec((1, LANES), const),
            pl.BlockSpec((1, LANES), const),
            pl.BlockSpec((1, SSD_INNER), const),
            pl.BlockSpec((1, SSD_INNER), const),
        ],
        out_specs=(pl.BlockSpec((1, rows, SSD_INNER), blk), pl.BlockSpec((1, hp, D_STATE), per_b)),
        out_shape=(jax.ShapeDtypeStruct((b, l, SSD_INNER), BF16),
                   jax.ShapeDtypeStruct((b, hp, D_STATE), F32)),
        scratch_shapes=[
            pltpu.VMEM((SUBLANES + CHUNK, CONV_DIM), F32),
            pltpu.VMEM((CHUNK, CONV_DIM), F32),
            pltpu.VMEM((CHUNK, LANES), F32),
            pltpu.VMEM((CHUNK, SSD_INNER), F32),
            pltpu.VMEM((hp, D_STATE), F32),
        ],
        compiler_params=_cparams(("arbitrary", "arbitrary")),
        name="ssd",
    )(xbc, z, dt_raw, h0, hist, conv_w, conv_b, dt_bias, a_neg, d_skip, norm_w)


def _outproj_kernel(x_ref, attn_ref, ssd_ref, anw_ref, woa_ref, wos_ref, fnw_ref,
                    rwh_ref, rwl_ref, rb_ref, x2_ref, xn_ref, comb_ref):
    a = attn_ref[...]
    an = ((a * _rms_scale(a)) * anw_ref[...]).astype(BF16)
    d = jnp.dot(an, woa_ref[...], preferred_element_type=F32)
    d = d + jnp.dot(ssd_ref[...], wos_ref[...], preferred_element_type=F32)
    x2 = x_ref[...] + d
    x2_ref[...] = x2
    xn = (x2 * _rms_scale(x2)) * fnw_ref[...]
    xh = xn.astype(BF16)
    xl = (xn - xh.astype(F32)).astype(BF16)
    xn_ref[...] = xh

    logits = (jnp.dot(xh, rwh_ref[...], preferred_element_type=F32)
              + jnp.dot(xh, rwl_ref[...], preferred_element_type=F32)
              + jnp.dot(xl, rwh_ref[...], preferred_element_type=F32)) + rb_ref[...]
    lane = lax.broadcasted_iota(jnp.int32, logits.shape, 1)
    lane_f = lane.astype(F32)
    ninf = jnp.float32(-jnp.inf)
    far = jnp.float32(2 * LANES)

    def first_max(vals):
        m = jnp.max(vals, axis=1, keepdims=True)
        idx = jnp.min(jnp.where(vals == m, lane_f, far), axis=1, keepdims=True)
        return m, idx

    g_mask = (lane >= N_EXPERTS) & (lane < N_EXPERTS + N_EXPERT_GROUPS)
    gl = jnp.where(g_mask, logits, ninf)
    g_max, g_idx = first_max(gl)
    g_w = 1.0 / jnp.sum(jnp.where(g_mask, jnp.exp(gl - g_max), 0.0), axis=1, keepdims=True)
    g_sel = g_idx - N_EXPERTS
    e_mask = (lane < N_EXPERTS) & ((lane // EXPERTS_PER_GROUP).astype(F32) == g_sel)
    el = jnp.where(e_mask, logits, ninf)
    m1, i1 = first_max(el)
    m2, i2 = first_max(jnp.where(lane_f == i1, ninf, el))
    ex = jnp.exp(m2 - m1)
    w1 = g_w / (1.0 + ex)
    w2 = g_w * (ex / (1.0 + ex))
    comb_ref[...] = jnp.where(lane_f == i1, w1, jnp.where(lane_f == i2, w2, 0.0))


def _out_proj(x, attn, ssd, attn_norm_w, wo_a, wo_s, ffn_norm_w, rw_hi, rw_lo, rb, *, tm, row0):
    n = attn.shape[0] - row0 * tm
    rows_in = lambda i: (i + row0, 0)
    rows_out = lambda i: (i, 0)
    const = lambda i: (0, 0)
    return pl.pallas_call(
        _outproj_kernel,
        grid=(n // tm,),
        in_specs=[
            pl.BlockSpec((tm, D_MODEL), rows_in),
            pl.BlockSpec((tm, ATTN_WIDTH), rows_in),
            pl.BlockSpec((tm, SSD_INNER), rows_in),
            pl.BlockSpec((1, ATTN_WIDTH), const),
            pl.BlockSpec((ATTN_WIDTH, D_MODEL), const),
            pl.BlockSpec((SSD_INNER, D_MODEL), const),
            pl.BlockSpec((1, D_MODEL), const),
            pl.BlockSpec((D_MODEL, LANES), const),
            pl.BlockSpec((D_MODEL, LANES), const),
            pl.BlockSpec((1, LANES), const),
        ],
        out_specs=(pl.BlockSpec((tm, D_MODEL), rows_out), pl.BlockSpec((tm, D_MODEL), rows_out),
                   pl.BlockSpec((tm, LANES), rows_out)),
        out_shape=(jax.ShapeDtypeStruct((n, D_MODEL), F32), jax.ShapeDtypeStruct((n, D_MODEL), BF16),
                   jax.ShapeDtypeStruct((n, LANES), F32)),
        compiler_params=_cparams(("arbitrary",)),
        name="out_proj_router",
    )(x, attn, ssd, attn_norm_w, wo_a, wo_s, ffn_norm_w, rw_hi, rw_lo, rb)


def _moe_kernel(xn_ref, comb_ref, x2_ref, wg_ref, wu_ref, wd_ref, fw_ref, y_ref, acc_ref):
    e = pl.program_id(1)

    @pl.when(e == 0)
    def _():
        acc_ref[...] = jnp.zeros_like(acc_ref)

    x = xn_ref[...]
    g = jnp.dot(x, wg_ref[0], preferred_element_type=F32)
    u = jnp.dot(x, wu_ref[0], preferred_element_type=F32)
    comb = comb_ref[...]
    lane = lax.broadcasted_iota(jnp.int32, comb.shape, 1)
    c = jnp.sum(jnp.where(lane == e, comb, 0.0), axis=1, keepdims=True)
    h = ((g * jax.nn.sigmoid(g)) * u * c).astype(BF16)
    acc_ref[...] += jnp.dot(h, wd_ref[0], preferred_element_type=F32)

    @pl.when(e == pl.num_programs(1) - 1)
    def _():
        y = x2_ref[...] + acc_ref[...]
        y_ref[...] = (y * _rms_scale(y)) * fw_ref[...]


def _moe(xn, comb, x2, w_gate, w_up, w_down, final_w, *, tm):
    n = xn.shape[0]
    rows = lambda i, e: (i, 0)
    per_e = lambda i, e: (e, 0, 0)
    return pl.pallas_call(
        _moe_kernel,
        grid=(n // tm, N_EXPERTS),
        in_specs=[
            pl.BlockSpec((tm, D_MODEL), rows),
            pl.BlockSpec((tm, LANES), rows),
            pl.BlockSpec((tm, D_MODEL), rows),
            pl.BlockSpec((1, D_MODEL, D_EXPERT), per_e),
            pl.BlockSpec((1, D_MODEL, D_EXPERT), per_e),
            pl.BlockSpec((1, D_EXPERT, D_MODEL), per_e),
            pl.BlockSpec((1, D_MODEL), lambda i, e: (0, 0)),
        ],
        out_specs=pl.BlockSpec((tm, D_MODEL), rows),
        out_shape=jax.ShapeDtypeStruct((n, D_MODEL), F32),
        scratch_shapes=[pltpu.VMEM((tm, D_MODEL), F32)],
        compiler_params=_cparams(("arbitrary", "arbitrary")),
        name="moe",
    )(xn, comb, x2, w_gate, w_up, w_down, final_w)


def _largest_tile(n, cap):
    best = LANES
    for m in range(LANES, cap + 1, LANES):
        if n % m == 0:
            best = m
    return best


def _permute_key_tiles(x):
    t = x.shape[0]
    x = x.reshape(t // KEY_TILE, SUBLANES, STREAM, HEAD_PAIRS, 2 * ATTN_HEAD_DIM)
    return x.transpose(3, 0, 2, 1, 4).reshape(HEAD_PAIRS, t // KEY_TILE, KEY_TILE, 2 * ATTN_HEAD_DIM)


def kernel(x_prompt, x_sample, cache_k, cache_v, page_table, state_ssm, state_conv, meta_tokens,
           norm_mix_w, w_in, attn_logit_bias, conv_w, conv_b, dt_bias, a_log, d_skip, ssd_norm_w,
           attn_norm_w, w_out, norm_ffn_w, router_group_w, router_group_b, router_expert_w,
           router_expert_b, expert_w_gate, expert_w_up, expert_w_down, final_norm_w):
    assert w_in.shape[0] == 1 and x_prompt.shape[0] == 1
    seq = x_prompt.shape[1]
    dec_batch, dec_seq, _ = x_sample.shape
    assert seq % Q_BLOCK == 0 and dec_seq == SUBLANES

    w_in0 = w_in[0]
    n_main = 3 * ATTN_WIDTH + SSD_INNER + CONV_DIM
    w_main = w_in0[:, :n_main].astype(BF16)
    w_dt = jnp.pad(w_in0[:, n_main:], ((0, 0), (0, LANES - SSD_HEADS))).astype(BF16)
    nmw = norm_mix_w[0][None]
    pad_heads = lambda v: jnp.pad(v, (0, LANES - SSD_HEADS))[None]
    dtb = pad_heads(dt_bias[0])
    a_neg = pad_heads(-jnp.exp(a_log[0]))
    dskip = jnp.repeat(d_skip[0], SSD_HEAD_DIM)[None]
    ssd_nw = ssd_norm_w[0][None]
    conv_w0, conv_b0 = conv_w[0], conv_b[0][None]
    wo = w_out[0].astype(BF16)
    wo_a, wo_s = wo[:ATTN_WIDTH], wo[ATTN_WIDTH:]
    rw = jnp.concatenate([router_expert_w[0], router_group_w[0]], axis=1)
    rw = jnp.pad(rw, ((0, 0), (0, LANES - rw.shape[1])))
    rw_hi = rw.astype(BF16)
    rw_lo = (rw - rw_hi.astype(F32)).astype(BF16)
    rb = jnp.concatenate([router_expert_b[0], router_group_b[0]])
    rb = jnp.pad(rb, (0, LANES - rb.shape[0]))[None]
    wg, wu, wd = (w[0].astype(BF16) for w in (expert_w_gate, expert_w_up, expert_w_down))
    anw, fnw, finw = attn_norm_w[0][None], norm_ffn_w[0][None], final_norm_w[None]
    bias = attn_logit_bias[0]

    def tail(x, attn, ssd, tm, row0):
        x2, xn, comb = _out_proj(x, attn, ssd, anw, wo_a, wo_s, fnw, rw_hi, rw_lo, rb, tm=tm, row0=row0)
        return _moe(xn, comb, x2, wg, wu, wd, finw, tm=_largest_tile(xn.shape[0], 512))

    xp = jnp.concatenate([jnp.zeros((PROMPT_PAD, D_MODEL), F32), meta_tokens.astype(F32), x_prompt[0]], axis=0)
    tp = xp.shape[0]
    q_p, k32_p, v32_p, kb_p, vb_p, z_p, xbc_p, dt_p = _in_proj(xp, nmw, w_main, w_dt, _largest_tile(tp, 640))
    k_perm = _permute_key_tiles(kb_p.reshape(tp, HEAD_PAIRS, 2 * ATTN_HEAD_DIM))
    v_t = _permute_key_tiles(vb_p.reshape(tp, HEAD_PAIRS, 2 * ATTN_HEAD_DIM)).transpose(0, 1, 3, 2)
    attn_p = _prompt_attention(bias, q_p.T, k_perm, v_t)
    hp = SSD_HEADS * SSD_HEAD_DIM
    ssd_p, h_p = _ssd(xbc_p[None], z_p[None], dt_p[None],
                      jnp.zeros((1, hp, D_STATE), F32), jnp.zeros((1, SUBLANES, CONV_DIM), F32),
                      conv_w0, conv_b0, dtb, a_neg, dskip, ssd_nw, rows=CHUNK, first_valid=PROMPT_PAD)
    y_prompt = tail(xp, attn_p, ssd_p[0], Q_BLOCK, 1)

    n_s = dec_batch * dec_seq
    xs = x_sample.reshape(n_s, D_MODEL)
    q_s, k32_s, v32_s, _, _, z_s, xbc_s, dt_s = _in_proj(xs, nmw, w_main, w_dt, _largest_tile(n_s, 512))
    q4 = q_s.reshape(dec_batch, dec_seq, ATTN_HEADS, ATTN_HEAD_DIM)
    eye = jnp.eye(ATTN_HEADS, dtype=BF16)
    q_bd = q4.transpose(0, 2, 1, 3)[:, :, :, None, :] * eye[None, :, None, :, None]
    q_bd = q_bd.reshape(dec_batch, ATTN_HEADS * dec_seq, ATTN_WIDTH)
    bias_rows = jnp.broadcast_to(jnp.repeat(bias, dec_seq)[:, None], (ATTN_HEADS * dec_seq, PAGE_SIZE))

    def pages_t(x):
        return jnp.moveaxis(x, -3, -1)

    def spread(x):
        x = pages_t(x.reshape(dec_batch, dec_seq, ATTN_HEADS, ATTN_HEAD_DIM))
        return jnp.pad(x, ((0, 0), (0, 0), (0, 0), (0, PAGE_SIZE - dec_seq)))

    attn_s = _sample_attention(page_table, q_bd, bias_rows, spread(k32_s), spread(v32_s),
                               pages_t(cache_k[0]), pages_t(cache_v[0]), dec_seq)
    hist_s = jnp.pad(state_conv[0], ((0, 0), (SUBLANES - (CONV_W - 1), 0), (0, 0)))
    ssd_s, h_s = _ssd(xbc_s.reshape(dec_batch, dec_seq, CONV_DIM), z_s.reshape(dec_batch, dec_seq, SSD_INNER),
                      dt_s.reshape(dec_batch, dec_seq, LANES),
                      state_ssm[0].reshape(dec_batch, hp, D_STATE), hist_s,
                      conv_w0, conv_b0, dtb, a_neg, dskip, ssd_nw, rows=dec_seq, first_valid=0)
    y_sample = tail(xs, attn_s.reshape(n_s, ATTN_WIDTH), ssd_s.reshape(n_s, SSD_INNER), _largest_tile(n_s, 256), 0)

    heads = (ATTN_HEADS, ATTN_HEAD_DIM)
    state = (SSD_HEADS, SSD_HEAD_DIM, D_STATE)
    xbc_s3 = xbc_s.reshape(dec_batch, dec_seq, CONV_DIM)
    return (
        y_prompt[None],
        y_sample.reshape(dec_batch, dec_seq, D_MODEL),
        k32_p[PROMPT_PAD:].reshape(1, 1, tp - PROMPT_PAD, *heads),
        v32_p[PROMPT_PAD:].reshape(1, 1, tp - PROMPT_PAD, *heads),
        h_p.reshape(1, 1, *state).astype(state_ssm.dtype),
        xbc_p[tp - (CONV_W - 1):][None, None],
        k32_s.reshape(1, dec_batch, dec_seq, *heads),
        v32_s.reshape(1, dec_batch, dec_seq, *heads),
        h_s.reshape(1, dec_batch, *state).astype(state_ssm.dtype),
        xbc_s3[:, dec_seq - (CONV_W - 1):][None],
    )
```
